```python
import jax, jax.numpy as jnp
from jax import lax
import numpy as np

D_MODEL = 1024
BATCH = 16
SEQ = 2048
DEPTH = 2
DEC_BATCH = 16
DEC_SEQ = 32
PAST_LEN = 4096

CHUNK = 64
EPS = 1e-6
D_CONV = D_MODEL
CONV_WIDTH = 31
D_INNER = 2 * D_MODEL
SSM_HEAD_DIM = 64
SSM_HEADS = D_INNER // SSM_HEAD_DIM
SSM_GROUPS = 8
SSM_HPG = SSM_HEADS // SSM_GROUPS
SSM_STATE = 128
SSM_CONV_WIDTH = 4
SSM_CONV_DIM = D_INNER + 2 * SSM_GROUPS * SSM_STATE
N_MEM = 256
MEM_HEADS = 4
MEM_HEAD_DIM = 128
D_MEM = MEM_HEADS * MEM_HEAD_DIM
N_BRANCH = 3
D_FF = ((8 * D_MODEL + 3 * 256 - 1) // (3 * 256)) * 256
N_GLU = 2 * D_CONV
N_Z = D_INNER
N_XBC = SSM_CONV_DIM
N_DT = SSM_HEADS
N_Q = D_MEM
N_GATE = N_BRANCH * D_MODEL
D_IN_PROJ = N_GLU + N_Z + N_XBC + N_DT + N_Q + N_GATE
IN_SPLITS = [N_GLU, N_GLU + N_Z, N_GLU + N_Z + N_XBC, N_GLU + N_Z + N_XBC + N_DT,
             N_GLU + N_Z + N_XBC + N_DT + N_Q]

kernel_name = 'hybrid_conformer_ssd_stream_step'


def rms_norm(x, g):
    xf = x.astype(jnp.float32)
    y = xf * lax.rsqrt(jnp.mean(xf * xf, axis=-1, keepdims=True) + EPS)
    return (y * g.astype(jnp.float32)).astype(x.dtype)


def layer_norm(x, g, b):
    xf = x.astype(jnp.float32)
    xc = xf - jnp.mean(xf, axis=-1, keepdims=True)
    var = jnp.mean(xc * xc, axis=-1, keepdims=True)
    y = xc * lax.rsqrt(var + EPS) * g.astype(jnp.float32) + b.astype(jnp.float32)
    return y.astype(x.dtype)


def causal_dwconv(u, hist, w, b):
    width = w.shape[0]
    upad = jnp.concatenate([hist.astype(u.dtype), u], axis=1)
    y = lax.conv_general_dilated(upad, w[:, None, :].astype(u.dtype), window_strides=(1,),
                                 padding='VALID', dimension_numbers=('NWC', 'WIO', 'NWC'),
                                 feature_group_count=u.shape[-1])
    return y + b.astype(u.dtype), upad[:, upad.shape[1] - (width - 1):]


def ssd_scan(x, dt, a, b_in, c_in, state0):
    bsz, seqlen = x.shape[0], x.shape[1]
    q = CHUNK if seqlen % CHUNK == 0 else seqlen
    nc = seqlen // q

    def blocks(t):
        return jnp.moveaxis(t.reshape((bsz, nc, q) + t.shape[2:]), 1, 0)

    xg = x.reshape(bsz, seqlen, SSM_GROUPS, SSM_HPG, SSM_HEAD_DIM)
    dtg = dt.reshape(bsz, seqlen, SSM_GROUPS, SSM_HPG)
    ag = a.reshape(SSM_GROUPS, SSM_HPG)
    causal = jnp.tril(jnp.ones((q, q), dtype=bool))[None, :, :, None, None]

    def step(state, inp):
        xc, dtc, bc, cc = inp
        acum = jnp.cumsum(dtc * ag, axis=1)
        seg = acum[:, :, None] - acum[:, None, :]
        lmat = jnp.exp(jnp.where(causal, seg, -jnp.inf))
        xdt = xc * dtc[..., None]
        cb = jnp.einsum('blgn,bsgn->blsg', cc, bc)
        y = jnp.einsum('blsg,blsgr,bsgrp->blgrp', cb, lmat, xdt)
        y = y + jnp.einsum('blgn,bgrpn,blgr->blgrp', cc, state, jnp.exp(acum))
        decay_end = jnp.exp(acum[:, -1:] - acum)
        new_state = (state * jnp.exp(acum[:, -1])[..., None, None]
                     + jnp.einsum('bsgn,bsgr,bsgrp->bgrpn', bc, decay_end, xdt))
        return new_state, y

    s0 = state0.reshape(bsz, SSM_GROUPS, SSM_HPG, SSM_HEAD_DIM, SSM_STATE)
    s_final, ys = lax.scan(step, s0, (blocks(xg), blocks(dtg), blocks(b_in), blocks(c_in)))
    y = jnp.moveaxis(ys, 0, 1).reshape(bsz, seqlen, SSM_HEADS, SSM_HEAD_DIM)
    return y, s_final.reshape(bsz, SSM_HEADS, SSM_HEAD_DIM, SSM_STATE)


def memory_kv(mem, g, wk, wv):
    mn = rms_norm(mem, g)
    bsz = mem.shape[0]
    k = (mn @ wk).reshape(bsz, N_MEM, MEM_HEADS, MEM_HEAD_DIM)
    v = (mn @ wv).reshape(bsz, N_MEM, MEM_HEADS, MEM_HEAD_DIM)
    return k, v


def token_mixer(h, mem_k, mem_v, conv_hist, ssm_conv_hist, ssm_state, lp):
    bsz, seqlen, _ = h.shape
    proj = h @ lp['w_in']
    glu_in, z, xbc, dt_raw, q, gates = jnp.split(proj, IN_SPLITS, axis=-1)

    u = glu_in[..., :D_CONV] * jax.nn.sigmoid(glu_in[..., D_CONV:])
    u, new_conv_hist = causal_dwconv(u, conv_hist, lp['dw_w'], lp['dw_b'])
    u = jax.nn.silu(layer_norm(u, lp['cln_g'], lp['cln_b']))
    out_conv = u @ lp['w_conv_out']

    xbc, new_ssm_conv_hist = causal_dwconv(xbc, ssm_conv_hist, lp['ssm_conv_w'], lp['ssm_conv_b'])
    xbc = jax.nn.silu(xbc.astype(jnp.float32))
    xs, bs, cs = jnp.split(xbc, [D_INNER, D_INNER + SSM_GROUPS * SSM_STATE], axis=-1)
    xs = xs.reshape(bsz, seqlen, SSM_HEADS, SSM_HEAD_DIM)
    bs = bs.reshape(bsz, seqlen, SSM_GROUPS, SSM_STATE)
    cs = cs.reshape(bsz, seqlen, SSM_GROUPS, SSM_STATE)
    dt = jax.nn.softplus(dt_raw.astype(jnp.float32) + lp['dt_bias'].astype(jnp.float32))
    a = -jnp.exp(lp['a_log'].astype(jnp.float32))
    y, new_state = ssd_scan(xs, dt, a, bs, cs, ssm_state.astype(jnp.float32))
    y = y + lp['d_skip'].astype(jnp.float32)[:, None] * xs
    y = y.reshape(bsz, seqlen, D_INNER) * jax.nn.silu(z.astype(jnp.float32))
    yg = y.reshape(bsz, seqlen, SSM_GROUPS, D_INNER // SSM_GROUPS)
    yg = yg * lax.rsqrt(jnp.mean(yg * yg, axis=-1, keepdims=True) + EPS)
    y = yg.reshape(bsz, seqlen, D_INNER) * lp['ssm_norm_g'].astype(jnp.float32)
    out_ssm = y.astype(h.dtype) @ lp['w_ssm_out']

    qh = q.reshape(bsz, seqlen, MEM_HEADS, MEM_HEAD_DIM)
    s = jnp.einsum('blhd,bmhd->bhlm', qh, mem_k.astype(h.dtype)).astype(jnp.float32) * (MEM_HEAD_DIM ** -0.5)
    att = jax.nn.softmax(s, axis=-1).astype(h.dtype)
    o = jnp.einsum('bhlm,bmhd->blhd', att, mem_v.astype(h.dtype)).reshape(bsz, seqlen, D_MEM)
    out_mem = o @ lp['w_mem_out']

    g_conv, g_ssm, g_mem = jnp.split(jax.nn.sigmoid(gates), N_BRANCH, axis=-1)
    merged = g_conv * out_conv + g_ssm * out_ssm + g_mem * out_mem
    return (merged @ lp['w_o'], new_conv_hist, new_ssm_conv_hist, new_state)


def trunk_layer(x, mem_k, mem_v, conv_hist, ssm_conv_hist, ssm_state, lp):
    mix, new_conv, new_ssm_conv, new_ssm = token_mixer(rms_norm(x, lp['g_mix_pre']), mem_k, mem_v,
                                                       conv_hist, ssm_conv_hist, ssm_state, lp)
    x = x + rms_norm(mix, lp['g_mix_post'])
    hn = rms_norm(x, lp['g_ffn_pre'])
    f = (jax.nn.silu(hn @ lp['w_gate']) * (hn @ lp['w_up'])) @ lp['w_down']
    x = x + rms_norm(f, lp['g_ffn_post'])
    return x, new_conv, new_ssm_conv, new_ssm


def setup_inputs(seed: int = 0) -> dict:
    key = jax.random.key(seed)
    keys = iter(jax.random.split(key, 48))

    def nrm(shape, scale):
        return jax.random.normal(next(keys), shape, jnp.float32) * scale

    def gain(shape):
        return 1.0 + nrm(shape, 0.02)

    dt0 = jnp.exp(jax.random.uniform(next(keys), (DEPTH, SSM_HEADS), jnp.float32,
                                     np.log(1e-3).astype(np.float32), np.log(1e-1).astype(np.float32)))
    dt_bias = dt0 + jnp.log(-jnp.expm1(-dt0))
    a_log = jnp.log(jax.random.uniform(next(keys), (DEPTH, SSM_HEADS), jnp.float32, 1.0, 16.0))
    return {
        'x_prompt': nrm((BATCH, SEQ, D_MODEL), 1.0),
        'x_sample': nrm((DEC_BATCH, DEC_SEQ, D_MODEL), 1.0),
        'mem_prompt': nrm((BATCH, N_MEM, D_MODEL), 1.0),
        'cache_mem_k': nrm((DEPTH, DEC_BATCH, N_MEM, MEM_HEADS, MEM_HEAD_DIM), 1.0),
        'cache_mem_v': nrm((DEPTH, DEC_BATCH, N_MEM, MEM_HEADS, MEM_HEAD_DIM), 1.0),
        'state_conv': nrm((DEPTH, DEC_BATCH, CONV_WIDTH - 1, D_CONV), 0.5),
        'state_ssm_conv': nrm((DEPTH, DEC_BATCH, SSM_CONV_WIDTH - 1, SSM_CONV_DIM), 1.0),
        'state_ssm': nrm((DEPTH, DEC_BATCH, SSM_HEADS, SSM_HEAD_DIM, SSM_STATE), 0.1),
        'g_mix_pre': gain((DEPTH, D_MODEL)),
        'w_in': nrm((DEPTH, D_MODEL, D_IN_PROJ), D_MODEL ** -0.5),
        'dw_w': nrm((DEPTH, CONV_WIDTH, D_CONV), CONV_WIDTH ** -0.5),
        'dw_b': nrm((DEPTH, D_CONV), 0.02),
        'cln_g': gain((DEPTH, D_CONV)),
        'cln_b': nrm((DEPTH, D_CONV), 0.02),
        'w_conv_out': nrm((DEPTH, D_CONV, D_MODEL), D_CONV ** -0.5),
        'ssm_conv_w': nrm((DEPTH, SSM_CONV_WIDTH, SSM_CONV_DIM), SSM_CONV_WIDTH ** -0.5),
        'ssm_conv_b': nrm((DEPTH, SSM_CONV_DIM), 0.02),
        'dt_bias': dt_bias,
        'a_log': a_log,
        'd_skip': gain((DEPTH, SSM_HEADS)),
        'ssm_norm_g': gain((DEPTH, D_INNER)),
        'w_ssm_out': nrm((DEPTH, D_INNER, D_MODEL), D_INNER ** -0.5),
        'mem_norm_g': gain((DEPTH, D_MODEL)),
        'w_mem_k': nrm((DEPTH, D_MODEL, D_MEM), D_MODEL ** -0.5),
        'w_mem_v': nrm((DEPTH, D_MODEL, D_MEM), D_MODEL ** -0.5),
        'w_mem_out': nrm((DEPTH, D_MEM, D_MODEL), D_MEM ** -0.5),
        'w_o': nrm((DEPTH, D_MODEL, D_MODEL), D_MODEL ** -0.5),
        'g_mix_post': gain((DEPTH, D_MODEL)),
        'g_ffn_pre': gain((DEPTH, D_MODEL)),
        'w_gate': nrm((DEPTH, D_MODEL, D_FF), D_MODEL ** -0.5),
        'w_up': nrm((DEPTH, D_MODEL, D_FF), D_MODEL ** -0.5),
        'w_down': nrm((DEPTH, D_FF, D_MODEL), D_FF ** -0.5),
        'g_ffn_post': gain((DEPTH, D_MODEL)),
    }


def reference(x_prompt, x_sample, mem_prompt, cache_mem_k, cache_mem_v, state_conv, state_ssm_conv,
              state_ssm, g_mix_pre, w_in, dw_w, dw_b, cln_g, cln_b, w_conv_out, ssm_conv_w, ssm_conv_b,
              dt_bias, a_log, d_skip, ssm_norm_g, w_ssm_out, mem_norm_g, w_mem_k, w_mem_v, w_mem_out,
              w_o, g_mix_post, g_ffn_pre, w_gate, w_up, w_down, g_ffn_post):
    bp = x_prompt.shape[0]
    xp, xs = x_prompt, x_sample
    mk_p, mv_p, conv_p, sconv_p, ssm_p = [], [], [], [], []
    conv_s, sconv_s, ssm_s = [], [], []
    for i in range(DEPTH):
        lp = {'g_mix_pre': g_mix_pre[i], 'w_in': w_in[i], 'dw_w': dw_w[i], 'dw_b': dw_b[i],
              'cln_g': cln_g[i], 'cln_b': cln_b[i], 'w_conv_out': w_conv_out[i],
              'ssm_conv_w': ssm_conv_w[i], 'ssm_conv_b': ssm_conv_b[i], 'dt_bias': dt_bias[i],
              'a_log': a_log[i], 'd_skip': d_skip[i], 'ssm_norm_g': ssm_norm_g[i],
              'w_ssm_out': w_ssm_out[i], 'w_mem_out': w_mem_out[i], 'w_o': w_o[i],
              'g_mix_post': g_mix_post[i], 'g_ffn_pre': g_ffn_pre[i], 'w_gate': w_gate[i],
              'w_up': w_up[i], 'w_down': w_down[i], 'g_ffn_post': g_ffn_post[i]}
        mk, mv = memory_kv(mem_prompt, mem_norm_g[i], w_mem_k[i], w_mem_v[i])
        zc = jnp.zeros((bp, CONV_WIDTH - 1, D_CONV), xp.dtype)
        zsc = jnp.zeros((bp, SSM_CONV_WIDTH - 1, SSM_CONV_DIM), xp.dtype)
        zs = jnp.zeros((bp, SSM_HEADS, SSM_HEAD_DIM, SSM_STATE), jnp.float32)
        xp, c_new, sc_new, s_new = trunk_layer(xp, mk, mv, zc, zsc, zs, lp)
        mk_p.append(mk)
        mv_p.append(mv)
        conv_p.append(c_new)
        sconv_p.append(sc_new)
        ssm_p.append(s_new.astype(xp.dtype))
        xs, c_new, sc_new, s_new = trunk_layer(xs, cache_mem_k[i], cache_mem_v[i], state_conv[i],
                                               state_ssm_conv[i], state_ssm[i], lp)
        conv_s.append(c_new)
        sconv_s.append(sc_new)
        ssm_s.append(s_new.astype(state_ssm.dtype))
    return (xp, xs, jnp.stack(mk_p), jnp.stack(mv_p), jnp.stack(conv_p), jnp.stack(sconv_p),
            jnp.stack(ssm_p), jnp.stack(conv_s), jnp.stack(sconv_s), jnp.stack(ssm_s))
```

```python
import functools

import jax
import jax.numpy as jnp
from jax import lax
from jax.experimental import pallas as pl
from jax.experimental.pallas import tpu as pltpu

F32 = jnp.float32
BF16 = jnp.bfloat16

EPS = 1e-6
D_MODEL = 1024
D_CONV = D_MODEL
CONV_WIDTH = 31
CONV_HIST = CONV_WIDTH - 1
D_INNER = 2 * D_MODEL
SSM_HEAD_DIM = 64
SSM_HEADS = D_INNER // SSM_HEAD_DIM
SSM_GROUPS = 8
SSM_HPG = SSM_HEADS // SSM_GROUPS
SSM_GROUP_DIM = SSM_HPG * SSM_HEAD_DIM
SSM_STATE = 128
SSM_CONV_WIDTH = 4
SSM_HIST = SSM_CONV_WIDTH - 1
SSM_CONV_DIM = D_INNER + 2 * SSM_GROUPS * SSM_STATE
SSD_BLOCK = 64
N_MEM = 256
MEM_HEADS = 4
MEM_HEAD_DIM = 128
D_MEM = MEM_HEADS * MEM_HEAD_DIM
D_FF = 2816
N_GLU = 2 * D_CONV
OFF_Z = N_GLU
OFF_XBC = OFF_Z + D_INNER
OFF_DT = OFF_XBC + SSM_CONV_DIM
OFF_Q = OFF_DT + SSM_HEADS
OFF_GATE = OFF_Q + D_MEM

SUBLANES = 8
CONV_PAD = 32
SSM_PAD = 8
ROW_BLOCK = 16
VMEM_LIMIT_BYTES = 56 * 1024 * 1024
NEG_BIG = -1e30


def _dot(a, b):
    return jnp.dot(a, b, preferred_element_type=F32)


def _dot_nt(a, b):
    return lax.dot_general(a, b, (((1,), (1,)), ((), ())), preferred_element_type=F32)


def _dot_tn(a, b):
    return lax.dot_general(a, b, (((0,), (0,)), ((), ())), preferred_element_type=F32)


def _rms(x, g):
    return x * lax.rsqrt(jnp.mean(x * x, axis=-1, keepdims=True) + EPS) * g


def _sigmoid(x):
    return 1.0 / (1.0 + jnp.exp(-x))


def _silu(x):
    return x * _sigmoid(x)


def _softplus(x):
    return jnp.maximum(x, 0.0) + jnp.log1p(jnp.exp(-jnp.abs(x)))


def _resident(shape):
    nd = len(shape)
    return pl.BlockSpec(shape, lambda *_: (0,) * nd, pipeline_mode=pl.Buffered(1))


def _params(sem):
    return pltpu.CompilerParams(dimension_semantics=sem, vmem_limit_bytes=VMEM_LIMIT_BYTES)


def _memkv_kernel(mem_ref, g_ref, w_ref, k_ref, v_ref, kb_ref, vb_ref):
    h = _rms(mem_ref[...], g_ref[...]).astype(BF16)
    kv = _dot(h, w_ref[...])
    k = kv[:, :D_MEM]
    v = kv[:, D_MEM:]
    k_ref[...] = k
    v_ref[...] = v
    kb_ref[...] = k.astype(BF16)
    vb_ref[...] = v.astype(BF16)


def _memory_kv(mem2d, g, w_kv):
    rows = mem2d.shape[0]
    tm = 512
    assert rows % tm == 0
    row_spec = pl.BlockSpec((tm, D_MODEL), lambda i: (i, 0))
    out_spec = pl.BlockSpec((tm, D_MEM), lambda i: (i, 0))
    return pl.pallas_call(
        _memkv_kernel,
        grid=(rows // tm,),
        in_specs=[row_spec, _resident((1, D_MODEL)), _resident((D_MODEL, 2 * D_MEM))],
        out_specs=[out_spec] * 4,
        out_shape=[jax.ShapeDtypeStruct((rows, D_MEM), F32)] * 2
        + [jax.ShapeDtypeStruct((rows, D_MEM), BF16)] * 2,
        compiler_params=_params(("arbitrary",)),
        name="memory_kv",
    )(mem2d, g, w_kv)


def _conv_mem_kernel(x_ref, hist_ref, k_ref, v_ref, g_ref, wglu_ref, wgc_ref, wq_ref, wgm_ref,
                     dww_ref, dwb_ref, clng_ref, clnb_ref, wco_ref, wmo_ref,
                     p_ref, hist_out_ref, ubuf, act):
    tl = x_ref.shape[0]
    t = pl.program_id(1)
    h = _rms(x_ref[...], g_ref[...]).astype(BF16)

    glu = _dot(h, wglu_ref[...])
    u = glu[:, :D_CONV] * _sigmoid(glu[:, D_CONV:])

    @pl.when(t == 0)
    def _():
        ubuf[CONV_PAD - CONV_HIST:CONV_PAD, :] = hist_ref[...]

    ubuf[CONV_PAD:CONV_PAD + tl, :] = u

    for r0 in range(0, tl, ROW_BLOCK):
        acc = jnp.broadcast_to(dwb_ref[...], (ROW_BLOCK, D_CONV))
        for k in range(CONV_WIDTH):
            s0 = r0 + CONV_PAD - CONV_HIST + k
            acc = acc + dww_ref[k:k + 1, :] * ubuf[s0:s0 + ROW_BLOCK, :]
        mu = jnp.mean(acc, axis=-1, keepdims=True)
        xc = acc - mu
        var = jnp.mean(xc * xc, axis=-1, keepdims=True)
        y = xc * lax.rsqrt(var + EPS) * clng_ref[...] + clnb_ref[...]
        act[r0:r0 + ROW_BLOCK, :] = _silu(y).astype(BF16)

    new_hist = ubuf[tl + CONV_PAD - CONV_HIST:tl + CONV_PAD, :]
    ubuf[CONV_PAD - CONV_HIST:CONV_PAD, :] = new_hist
    hist_out_ref[...] = new_hist

    out_conv = _dot(act[...], wco_ref[...])
    merged = _sigmoid(_dot(h, wgc_ref[...])) * out_conv

    q = _dot(h, wq_ref[...]).astype(BF16)
    heads = []
    for hd in range(MEM_HEADS):
        sl = slice(hd * MEM_HEAD_DIM, (hd + 1) * MEM_HEAD_DIM)
        s = _dot_nt(q[:, sl], k_ref[:, sl]) * (MEM_HEAD_DIM ** -0.5)
        e = jnp.exp(s - jnp.max(s, axis=-1, keepdims=True))
        att = e / jnp.sum(e, axis=-1, keepdims=True)
        heads.append(_dot(att.astype(BF16), v_ref[:, sl]))
    o = jnp.concatenate(heads, axis=-1).astype(BF16)
    out_mem = _dot(o, wmo_ref[...])
    merged = merged + _sigmoid(_dot(h, wgm_ref[...])) * out_mem
    p_ref[...] = merged


def _conv_mem_branch(x, hist, k_b, v_b, w, tl):
    bsz, seqlen, _ = x.shape
    assert seqlen % tl == 0 and tl % ROW_BLOCK == 0 and tl >= CONV_HIST
    tile = pl.BlockSpec((None, tl, D_MODEL), lambda b, t: (b, t, 0))
    per_batch = lambda r, c: pl.BlockSpec((None, r, c), lambda b, t: (b, 0, 0))
    weights = [w['g_mix_pre'], w['w_glu'], w['w_gate_conv'], w['w_q'], w['w_gate_mem'],
               w['dw_w'], w['dw_b'], w['cln_g'], w['cln_b'], w['w_conv_out'], w['w_mem_out']]
    return pl.pallas_call(
        _conv_mem_kernel,
        grid=(bsz, seqlen // tl),
        in_specs=[tile, per_batch(CONV_HIST, D_CONV), per_batch(N_MEM, D_MEM), per_batch(N_MEM, D_MEM)]
        + [_resident(a.shape) for a in weights],
        out_specs=[tile, per_batch(CONV_HIST, D_CONV)],
        out_shape=[jax.ShapeDtypeStruct(x.shape, F32),
                   jax.ShapeDtypeStruct((bsz, CONV_HIST, D_CONV), F32)],
        scratch_shapes=[pltpu.VMEM((CONV_PAD + tl, D_CONV), F32),
                        pltpu.VMEM((tl, D_CONV), BF16)],
        compiler_params=_params(("arbitrary", "arbitrary")),
        name="conv_mem_branch",
    )(x, hist, k_b, v_b, *weights)


def _ssd_merge_kernel(q, x_ref, p_ref, hist_ref, state_in_ref, g_ref, wz_ref, wxbc_ref, wdt_ref, wdtt_ref,
                      wgs_ref, cw_ref, cb_ref, dtb_row_ref, dtb_col_ref, alog_row_ref, alog_col_ref,
                      dskip_ref, ng_ref, wso_ref, wo_ref, gpost_ref,
                      xo_ref, hist_out_ref, state_ref, cbuf, h_s, xs_s, b_s, c_s, y_s):
    tl = x_ref.shape[0]
    t = pl.program_id(1)
    x = x_ref[...]
    h = _rms(x, g_ref[...]).astype(BF16)
    h_s[...] = h

    @pl.when(t == 0)
    def _():
        cbuf[SSM_PAD - SSM_HIST:SSM_PAD, :] = hist_ref[...]
        state_ref[...] = state_in_ref[...]

    cbuf[SSM_PAD:SSM_PAD + tl, :] = _dot(h, wxbc_ref[...])

    for r0 in range(0, tl, ROW_BLOCK):
        for c0 in range(0, SSM_CONV_DIM, SSM_GROUP_DIM):
            cs = slice(c0, c0 + SSM_GROUP_DIM)
            acc = jnp.broadcast_to(cb_ref[:, cs], (ROW_BLOCK, SSM_GROUP_DIM))
            for k in range(SSM_CONV_WIDTH):
                s0 = r0 + SSM_PAD - SSM_HIST + k
                acc = acc + cw_ref[k:k + 1, cs] * cbuf[s0:s0 + ROW_BLOCK, cs]
            acc = _silu(acc)
            if c0 < D_INNER:
                xs_s[c0 // SSM_GROUP_DIM, r0:r0 + ROW_BLOCK, :] = acc
            else:
                n0 = c0 - D_INNER
                for j in range(SSM_GROUP_DIM // SSM_STATE):
                    gi = (n0 // SSM_STATE + j) % SSM_GROUPS
                    dst = b_s if n0 + j * SSM_STATE < SSM_GROUPS * SSM_STATE else c_s
                    dst[gi, r0:r0 + ROW_BLOCK, :] = acc[:, j * SSM_STATE:(j + 1) * SSM_STATE].astype(BF16)

    new_hist = cbuf[tl + SSM_PAD - SSM_HIST:tl + SSM_PAD, :]
    cbuf[SSM_PAD - SSM_HIST:SSM_PAD, :] = new_hist
    hist_out_ref[...] = new_hist

    a_row = -jnp.exp(alog_row_ref[...])
    a_col = -jnp.exp(alog_col_ref[...])
    li = lax.broadcasted_iota(jnp.int32, (q, q), 0)
    si = lax.broadcasted_iota(jnp.int32, (q, q), 1)
    causal = li >= si
    tri = causal.astype(F32)
    tri_t = (si >= li).astype(F32)

    def block(c, carry):
        r0 = pl.multiple_of(c * q, q)
        hc = h_s[pl.ds(r0, q), :]
        dt = _softplus(_dot(hc, wdt_ref[...]) + dtb_row_ref[...])
        dt_t = _softplus(_dot_nt(wdtt_ref[...], hc) + dtb_col_ref[...])
        acum = jnp.dot(tri, dt * a_row, preferred_element_type=F32, precision=lax.Precision.HIGHEST)
        acum_t = jnp.dot(dt_t * a_col, tri_t, preferred_element_type=F32, precision=lax.Precision.HIGHEST)
        total = acum[q - 1:q, :]
        exp_a = jnp.exp(acum)
        decay_end = jnp.exp(total - acum)
        exp_total_t = jnp.exp(acum_t[:, q - 1:q])
        for g in range(SSM_GROUPS):
            cg = c_s[g, pl.ds(r0, q), :]
            bg = b_s[g, pl.ds(r0, q), :]
            xg = xs_s[g, pl.ds(r0, q), :]
            st = state_ref[g]
            cb = _dot_nt(cg, bg)
            y_inter = _dot_nt(cg, st.astype(BF16))
            ys, xds, decays = [], [], []
            for r in range(SSM_HPG):
                hh = g * SSM_HPG + r
                ps = slice(r * SSM_HEAD_DIM, (r + 1) * SSM_HEAD_DIM)
                seg = acum[:, hh:hh + 1] - acum_t[hh:hh + 1, :]
                lmat = jnp.exp(jnp.where(causal, seg, NEG_BIG))
                xh = xg[:, ps]
                xdt = xh * dt[:, hh:hh + 1]
                y = _dot((cb * lmat).astype(BF16), xdt.astype(BF16))
                y = y + exp_a[:, hh:hh + 1] * y_inter[:, ps]
                y = y + dskip_ref[:, g * SSM_GROUP_DIM + r * SSM_HEAD_DIM:g * SSM_GROUP_DIM + (r + 1) * SSM_HEAD_DIM] * xh
                ys.append(y)
                xds.append(xdt * decay_end[:, hh:hh + 1])
                decays.append(jnp.broadcast_to(exp_total_t[hh:hh + 1, :], (SSM_HEAD_DIM, SSM_STATE)))
            y_s[pl.ds(r0, q), g * SSM_GROUP_DIM:(g + 1) * SSM_GROUP_DIM] = jnp.concatenate(ys, axis=-1)
            xd = jnp.concatenate(xds, axis=-1).astype(BF16)
            state_ref[g] = st * jnp.concatenate(decays, axis=0) + _dot_tn(xd, bg)
        return carry

    lax.fori_loop(0, tl // q, block, 0)

    y = y_s[...] * _silu(_dot(h, wz_ref[...]))
    parts = []
    for g in range(SSM_GROUPS):
        yg = y[:, g * SSM_GROUP_DIM:(g + 1) * SSM_GROUP_DIM]
        parts.append(yg * lax.rsqrt(jnp.mean(yg * yg, axis=-1, keepdims=True) + EPS))
    y = (jnp.concatenate(parts, axis=-1) * ng_ref[...]).astype(BF16)
    out_ssm = _dot(y, wso_ref[...])
    merged = p_ref[...] + _sigmoid(_dot(h, wgs_ref[...])) * out_ssm
    mix = _dot(merged.astype(BF16), wo_ref[...])
    xo_ref[...] = x + _rms(mix, gpost_ref[...])


def _ssd_merge_branch(x, p, hist, state, w, tl, q):
    bsz, seqlen, _ = x.shape
    assert seqlen % tl == 0 and tl % q == 0 and tl % ROW_BLOCK == 0
    tile = pl.BlockSpec((None, tl, D_MODEL), lambda b, t: (b, t, 0))
    per_batch = lambda r, c: pl.BlockSpec((None, r, c), lambda b, t: (b, 0, 0))
    state_spec = pl.BlockSpec((None, SSM_GROUPS, SSM_GROUP_DIM, SSM_STATE), lambda b, t: (b, 0, 0, 0))
    weights = [w['g_mix_pre'], w['w_z'], w['w_xbc'], w['w_dt'], w['w_dt_t'], w['w_gate_ssm'],
               w['ssm_conv_w'], w['ssm_conv_b'], w['dt_bias_row'], w['dt_bias_col'], w['a_log_row'],
               w['a_log_col'], w['d_skip_wide'], w['ssm_norm_g'], w['w_ssm_out'], w['w_o'], w['g_mix_post']]
    return pl.pallas_call(
        functools.partial(_ssd_merge_kernel, q),
        grid=(bsz, seqlen // tl),
        in_specs=[tile, tile, per_batch(SSM_HIST, SSM_CONV_DIM), state_spec]
        + [_resident(a.shape) for a in weights],
        out_specs=[tile, per_batch(SSM_HIST, SSM_CONV_DIM), state_spec],
        out_shape=[jax.ShapeDtypeStruct(x.shape, F32),
                   jax.ShapeDtypeStruct((bsz, SSM_HIST, SSM_CONV_DIM), F32),
                   jax.ShapeDtypeStruct((bsz, SSM_GROUPS, SSM_GROUP_DIM, SSM_STATE), F32)],
        scratch_shapes=[pltpu.VMEM((SSM_PAD + tl, SSM_CONV_DIM), F32),
                        pltpu.VMEM((tl, D_MODEL), BF16),
                        pltpu.VMEM((SSM_GROUPS, tl, SSM_GROUP_DIM), F32),
                        pltpu.VMEM((SSM_GROUPS, tl, SSM_STATE), BF16),
                        pltpu.VMEM((SSM_GROUPS, tl, SSM_STATE), BF16),
                        pltpu.VMEM((tl, D_INNER), F32)],
        compiler_params=_params(("arbitrary", "arbitrary")),
        name="ssd_merge_branch",
    )(x, p, hist, state, *weights)


def _ffn_kernel(x_ref, gpre_ref, wg_ref, wu_ref, wd_ref, gpost_ref, o_ref):
    x = x_ref[...]
    hn = _rms(x, gpre_ref[...]).astype(BF16)
    a = (_silu(_dot(hn, wg_ref[...])) * _dot(hn, wu_ref[...])).astype(BF16)
    o_ref[...] = x + _rms(_dot(a, wd_ref[...]), gpost_ref[...])


def _ffn(x2d, w, tm):
    rows = x2d.shape[0]
    assert rows % tm == 0
    tile = pl.BlockSpec((tm, D_MODEL), lambda i: (i, 0))
    weights = [w['g_ffn_pre'], w['w_gate'], w['w_up'], w['w_down'], w['g_ffn_post']]
    return pl.pallas_call(
        _ffn_kernel,
        grid=(rows // tm,),
        in_specs=[tile] + [_resident(a.shape) for a in weights],
        out_specs=tile,
        out_shape=jax.ShapeDtypeStruct(x2d.shape, F32),
        compiler_params=_params(("arbitrary",)),
        name="ffn",
    )(x2d, *weights)


def _layer(x, k_b, v_b, conv_hist, ssm_hist, ssm_state, w, tl, q):
    bsz, seqlen, _ = x.shape
    p, new_conv = _conv_mem_branch(x, conv_hist, k_b, v_b, w, tl)
    state = ssm_state.reshape(bsz, SSM_GROUPS, SSM_GROUP_DIM, SSM_STATE)
    x1, new_ssm_conv, new_state = _ssd_merge_branch(x, p, ssm_hist, state, w, tl, q)
    x2 = _ffn(x1.reshape(bsz * seqlen, D_MODEL), w, min(512, bsz * seqlen)).reshape(x.shape)
    return x2, new_conv, new_ssm_conv, new_state.reshape(bsz, SSM_HEADS, SSM_HEAD_DIM, SSM_STATE)


def _layer_weights(i, g_mix_pre, w_in, dw_w, dw_b, cln_g, cln_b, w_conv_out, ssm_conv_w, ssm_conv_b,
                   dt_bias, a_log, d_skip, ssm_norm_g, w_ssm_out, w_mem_out, w_o, g_mix_post, g_ffn_pre,
                   w_gate, w_up, w_down, g_ffn_post):
    row = lambda a: a[i].reshape(1, -1).astype(F32)
    col = lambda a: a[i].reshape(-1, 1).astype(F32)
    win = w_in[i].astype(BF16)
    return {
        'g_mix_pre': row(g_mix_pre),
        'w_glu': win[:, :N_GLU],
        'w_z': win[:, OFF_Z:OFF_XBC],
        'w_xbc': win[:, OFF_XBC:OFF_DT],
        'w_dt': win[:, OFF_DT:OFF_Q],
        'w_dt_t': win[:, OFF_DT:OFF_Q].T,
        'w_q': win[:, OFF_Q:OFF_GATE],
        'w_gate_conv': win[:, OFF_GATE:OFF_GATE + D_MODEL],
        'w_gate_ssm': win[:, OFF_GATE + D_MODEL:OFF_GATE + 2 * D_MODEL],
        'w_gate_mem': win[:, OFF_GATE + 2 * D_MODEL:],
        'dw_w': dw_w[i].astype(F32), 'dw_b': row(dw_b), 'cln_g': row(cln_g), 'cln_b': row(cln_b),
        'w_conv_out': w_conv_out[i].astype(BF16),
        'ssm_conv_w': ssm_conv_w[i].astype(F32), 'ssm_conv_b': row(ssm_conv_b),
        'dt_bias_row': row(dt_bias), 'dt_bias_col': col(dt_bias),
        'a_log_row': row(a_log), 'a_log_col': col(a_log),
        'd_skip_wide': jnp.repeat(d_skip[i].astype(F32), SSM_HEAD_DIM).reshape(1, D_INNER),
        'ssm_norm_g': row(ssm_norm_g),
        'w_ssm_out': w_ssm_out[i].astype(BF16),
        'w_mem_out': w_mem_out[i].astype(BF16),
        'w_o': w_o[i].astype(BF16),
        'g_mix_post': row(g_mix_post), 'g_ffn_pre': row(g_ffn_pre),
        'w_gate': w_gate[i].astype(BF16), 'w_up': w_up[i].astype(BF16), 'w_down': w_down[i].astype(BF16),
        'g_ffn_post': row(g_ffn_post),
    }


def kernel(x_prompt, x_sample, mem_prompt, cache_mem_k, cache_mem_v, state_conv, state_ssm_conv, state_ssm, g_mix_pre, w_in, dw_w, dw_b, cln_g, cln_b, w_conv_out, ssm_conv_w, ssm_conv_b, dt_bias, a_log, d_skip, ssm_norm_g, w_ssm_out, mem_norm_g, w_mem_k, w_mem_v, w_mem_out, w_o, g_mix_post, g_ffn_pre, w_gate, w_up, w_down, g_ffn_post):
    depth = w_in.shape[0]
    bp, lp, _ = x_prompt.shape
    bs, ls, _ = x_sample.shape
    q_p = SSD_BLOCK if lp % SSD_BLOCK == 0 else lp
    q_s = SSD_BLOCK if ls % SSD_BLOCK == 0 else ls
    tl_p = 256 if lp % 256 == 0 else q_p
    tl_s = 256 if ls % 256 == 0 else q_s
    xp, xs = x_prompt, x_sample
    mem2d = mem_prompt.reshape(bp * N_MEM, D_MODEL)
    zc = jnp.zeros((bp, CONV_HIST, D_CONV), F32)
    zsc = jnp.zeros((bp, SSM_HIST, SSM_CONV_DIM), F32)
    zs = jnp.zeros((bp, SSM_HEADS, SSM_HEAD_DIM, SSM_STATE), F32)
    outs = [[] for _ in range(8)]
    for i in range(depth):
        w = _layer_weights(i, g_mix_pre, w_in, dw_w, dw_b, cln_g, cln_b, w_conv_out, ssm_conv_w, ssm_conv_b,
                           dt_bias, a_log, d_skip, ssm_norm_g, w_ssm_out, w_mem_out, w_o, g_mix_post,
                           g_ffn_pre, w_gate, w_up, w_down, g_ffn_post)
        w_kv = jnp.concatenate([w_mem_k[i], w_mem_v[i]], axis=1).astype(BF16)
        mk, mv, mk_b, mv_b = _memory_kv(mem2d, mem_norm_g[i].reshape(1, D_MODEL).astype(F32), w_kv)
        xp, c_new, sc_new, s_new = _layer(xp, mk_b.reshape(bp, N_MEM, D_MEM), mv_b.reshape(bp, N_MEM, D_MEM),
                                          zc, zsc, zs, w, tl_p, q_p)
        outs[0].append(mk.reshape(bp, N_MEM, MEM_HEADS, MEM_HEAD_DIM))
        outs[1].append(mv.reshape(bp, N_MEM, MEM_HEADS, MEM_HEAD_DIM))
        outs[2].append(c_new)
        outs[3].append(sc_new)
        outs[4].append(s_new)
        ck = cache_mem_k[i].reshape(bs, N_MEM, D_MEM).astype(BF16)
        cv = cache_mem_v[i].reshape(bs, N_MEM, D_MEM).astype(BF16)
        xs, c_new, sc_new, s_new = _layer(xs, ck, cv, state_conv[i], state_ssm_conv[i], state_ssm[i], w, tl_s, q_s)
        outs[5].append(c_new)
        outs[6].append(sc_new)
        outs[7].append(s_new)
    return (xp, xs) + tuple(jnp.stack(o) for o in outs)
```

```python
import functools

import jax
import jax.numpy as jnp
from jax import lax
from jax.experimental import pallas as pl
from jax.experimental.pallas import tpu as pltpu

F32 = jnp.float32
BF16 = jnp.bfloat16

EPS = 1e-6
D_MODEL = 1024
D_CONV = D_MODEL
CONV_WIDTH = 31
CONV_HIST = CONV_WIDTH - 1
D_INNER = 2 * D_MODEL
SSM_HEAD_DIM = 64
SSM_HEADS = D_INNER // SSM_HEAD_DIM
SSM_GROUPS = 8
SSM_HPG = SSM_HEADS // SSM_GROUPS
SSM_GROUP_DIM = SSM_HPG * SSM_HEAD_DIM
SSM_STATE = 128
SSM_CONV_WIDTH = 4
SSM_HIST = SSM_CONV_WIDTH - 1
SSM_CONV_DIM = D_INNER + 2 * SSM_GROUPS * SSM_STATE
SSD_BLOCK = 64
N_MEM = 256
MEM_HEADS = 4
MEM_HEAD_DIM = 128
D_MEM = MEM_HEADS * MEM_HEAD_DIM
D_FF = 2816
N_GLU = 2 * D_CONV
OFF_Z = N_GLU
OFF_XBC = OFF_Z + D_INNER
OFF_DT = OFF_XBC + SSM_CONV_DIM
OFF_Q = OFF_DT + SSM_HEADS
OFF_GATE = OFF_Q + D_MEM

SUBLANES = 8
LANES = 128
CONV_PAD = 32
SSM_PAD = 8
ROW_BLOCK = 32
CONV_STEP = 4
VMEM_LIMIT_BYTES = 56 * 1024 * 1024
NEG_BIG = -1e30


def _dot(a, b):
    return jnp.dot(a, b, preferred_element_type=F32)


def _dot_nt(a, b):
    return lax.dot_general(a, b, (((1,), (1,)), ((), ())), preferred_element_type=F32)


def _dot_tn(a, b):
    return lax.dot_general(a, b, (((0,), (0,)), ((), ())), preferred_element_type=F32)


def _rms(x, g):
    return x * lax.rsqrt(jnp.mean(x * x, axis=-1, keepdims=True) + EPS) * g


def _sigmoid(x):
    return 1.0 / (1.0 + jnp.exp(-x))


def _silu(x):
    return x * _sigmoid(x)


def _softplus(x):
    return jnp.maximum(x, 0.0) + jnp.log1p(jnp.exp(-jnp.abs(x)))


def _bf16_pieces(x):
    hi = x.astype(BF16)
    r1 = x - hi.astype(F32)
    mid = r1.astype(BF16)
    lo = (r1 - mid.astype(F32)).astype(BF16)
    return [hi, mid, lo]


def _resident(shape):
    nd = len(shape)
    return pl.BlockSpec(shape, lambda *_: (0,) * nd, pipeline_mode=pl.Buffered(1))


def _params(sem):
    return pltpu.CompilerParams(dimension_semantics=sem, vmem_limit_bytes=VMEM_LIMIT_BYTES)


def _memkv_kernel(mem_ref, g_ref, w_ref, k_ref, v_ref, kb_ref, vb_ref):
    h = _rms(mem_ref[...], g_ref[...]).astype(BF16)
    kv = _dot(h, w_ref[...])
    k = kv[:, :D_MEM]
    v = kv[:, D_MEM:]
    k_ref[...] = k
    v_ref[...] = v
    kb_ref[...] = k.astype(BF16)
    vb_ref[...] = v.astype(BF16)


def _memory_kv(mem2d, g, w_kv):
    rows = mem2d.shape[0]
    tm = 512
    assert rows % tm == 0
    row_spec = pl.BlockSpec((tm, D_MODEL), lambda i: (i, 0))
    out_spec = pl.BlockSpec((tm, D_MEM), lambda i: (i, 0))
    return pl.pallas_call(
        _memkv_kernel,
        grid=(rows // tm,),
        in_specs=[row_spec, _resident((1, D_MODEL)), _resident((D_MODEL, 2 * D_MEM))],
        out_specs=[out_spec] * 4,
        out_shape=[jax.ShapeDtypeStruct((rows, D_MEM), F32)] * 2
        + [jax.ShapeDtypeStruct((rows, D_MEM), BF16)] * 2,
        compiler_params=_params(("arbitrary",)),
        name="memory_kv",
    )(mem2d, g, w_kv)


def _conv_span(tl):
    blk = SUBLANES * CONV_STEP
    return 2 * blk if tl % (2 * blk) == 0 else blk


def _strided_depthwise(src, put, lane_tile, w_ref, width, first_row, r0, span, finish):
    blk = SUBLANES * CONV_STEP
    starts = [s0 + j for s0 in range(r0, r0 + span, blk) for j in range(CONV_STEP)]
    accs = [w_ref[lane_tile, width]] * len(starts)
    for k in range(width):
        wk = w_ref[lane_tile, k]
        accs = [a + wk * src[lane_tile, pl.ds(first_row + s0 + k, SUBLANES, stride=CONV_STEP), :]
                for a, s0 in zip(accs, starts)]
    for a, s0 in zip(accs, starts):
        put(lane_tile, pl.ds(s0, SUBLANES, stride=CONV_STEP), finish(a))


def _tap_table(w, b):
    wb = jnp.concatenate([w, b[None, :]], axis=0).astype(F32)
    wb = wb.reshape(wb.shape[0], -1, 1, LANES).transpose(1, 0, 2, 3)
    return jnp.broadcast_to(wb, wb.shape[:2] + (SUBLANES, LANES))


def _conv_mem_kernel(x_ref, hist_ref, k_ref, v_ref, g_ref, wglu_ref, wgc_ref, wq_ref, wgm_ref,
                     dww_ref, clng_ref, clnb_ref, wco_ref, wmo_ref,
                     p_ref, hist_out_ref, ubuf, ybuf, act):
    tl = x_ref.shape[0]
    t = pl.program_id(1)
    n_tiles = D_CONV // LANES
    h = _rms(x_ref[...], g_ref[...]).astype(BF16)

    glu = _dot(h, wglu_ref[...])
    u = glu[:, :D_CONV] * _sigmoid(glu[:, D_CONV:])

    @pl.when(t == 0)
    def _():
        for ct in range(n_tiles):
            ubuf[ct, CONV_PAD - CONV_HIST:CONV_PAD, :] = hist_ref[:, ct * LANES:(ct + 1) * LANES]

    def put_y(ct, rows, val):
        ybuf[ct, rows, :] = val

    for ct in range(n_tiles):
        ubuf[ct, CONV_PAD:CONV_PAD + tl, :] = u[:, ct * LANES:(ct + 1) * LANES]
    span = _conv_span(tl)
    for r0 in range(0, tl, span):
        for ct in range(n_tiles):
            _strided_depthwise(ubuf, put_y, ct, dww_ref, CONV_WIDTH, CONV_PAD - CONV_HIST, r0, span, lambda a: a)
        for r1 in range(r0, r0 + span, ROW_BLOCK):
            y = jnp.concatenate([ybuf[ct, r1:r1 + ROW_BLOCK, :] for ct in range(n_tiles)], axis=-1)
            mu = jnp.mean(y, axis=-1, keepdims=True)
            yc = y - mu
            var = jnp.mean(yc * yc, axis=-1, keepdims=True)
            yn = yc * lax.rsqrt(var + EPS) * clng_ref[...] + clnb_ref[...]
            act[r1:r1 + ROW_BLOCK, :] = _silu(yn).astype(BF16)
    for ct in range(n_tiles):
        new_hist = ubuf[ct, tl + CONV_PAD - CONV_HIST:tl + CONV_PAD, :]
        ubuf[ct, CONV_PAD - CONV_HIST:CONV_PAD, :] = new_hist
        hist_out_ref[:, ct * LANES:(ct + 1) * LANES] = new_hist

    out_conv = _dot(act[...], wco_ref[...])
    merged = _sigmoid(_dot(h, wgc_ref[...])) * out_conv

    q = _dot(h, wq_ref[...]).astype(BF16)
    heads = []
    for hd in range(MEM_HEADS):
        sl = slice(hd * MEM_HEAD_DIM, (hd + 1) * MEM_HEAD_DIM)
        s = _dot_nt(q[:, sl], k_ref[:, sl]) * (MEM_HEAD_DIM ** -0.5)
        e = jnp.exp(s - jnp.max(s, axis=-1, keepdims=True))
        heads.append(_dot(e.astype(BF16), v_ref[:, sl]) / jnp.sum(e, axis=-1, keepdims=True))
    o = jnp.concatenate(heads, axis=-1).astype(BF16)
    out_mem = _dot(o, wmo_ref[...])
    merged = merged + _sigmoid(_dot(h, wgm_ref[...])) * out_mem
    p_ref[...] = merged


def _conv_mem_branch(x, hist, k_b, v_b, w, tl):
    bsz, seqlen, _ = x.shape
    assert seqlen % tl == 0 and tl % ROW_BLOCK == 0 and tl >= CONV_HIST and tl % (SUBLANES * CONV_STEP) == 0
    tile = pl.BlockSpec((None, tl, D_MODEL), lambda b, t: (b, t, 0))
    per_batch = lambda r, c: pl.BlockSpec((None, r, c), lambda b, t: (b, 0, 0))
    weights = [w['g_mix_pre'], w['w_glu'], w['w_gate_conv'], w['w_q'], w['w_gate_mem'],
               w['dw_taps'], w['cln_g'], w['cln_b'], w['w_conv_out'], w['w_mem_out']]
    return pl.pallas_call(
        _conv_mem_kernel,
        grid=(bsz, seqlen // tl),
        in_specs=[tile, per_batch(CONV_HIST, D_CONV), per_batch(N_MEM, D_MEM), per_batch(N_MEM, D_MEM)]
        + [_resident(a.shape) for a in weights],
        out_specs=[tile, per_batch(CONV_HIST, D_CONV)],
        out_shape=[jax.ShapeDtypeStruct(x.shape, F32),
                   jax.ShapeDtypeStruct((bsz, CONV_HIST, D_CONV), F32)],
        scratch_shapes=[pltpu.VMEM((D_CONV // LANES, CONV_PAD + tl, LANES), F32),
                        pltpu.VMEM((D_CONV // LANES, tl, LANES), F32),
                        pltpu.VMEM((tl, D_CONV), BF16)],
        compiler_params=_params(("arbitrary", "arbitrary")),
        name="conv_mem_branch",
    )(x, hist, k_b, v_b, *weights)


def _ssd_merge_kernel(q, x_ref, p_ref, hist_ref, state_in_ref, g_ref, wz_ref, wxbc_ref, wdt_ref, wdtt_ref,
                      wgs_ref, cw_ref, dtb_row_ref, dtb_col_ref, alog_row_ref, alog_col_ref,
                      dskip_ref, ng_ref, wso_ref, wo_ref, gpost_ref, tri3_ref, e3_ref,
                      xo_ref, hist_out_ref, state_ref, cbuf, xs_s, b_s, c_s, y_s, et_s, spread_s):
    tl = x_ref.shape[0]
    t = pl.program_id(1)
    n_tiles = SSM_CONV_DIM // LANES
    n_x_tiles = D_INNER // LANES
    n_b_tiles = SSM_GROUPS * SSM_STATE // LANES
    n_blocks = tl // q
    slot = SSM_HEAD_DIM
    x = x_ref[...]
    h = _rms(x, g_ref[...]).astype(BF16)

    @pl.when(t == 0)
    def _():
        for ct in range(n_tiles):
            cbuf[ct, SSM_PAD - SSM_HIST:SSM_PAD, :] = hist_ref[:, ct * LANES:(ct + 1) * LANES]
        state_ref[...] = state_in_ref[...]

    def put_xbc(ct, rows, val):
        if ct < n_x_tiles:
            xs_s[ct, rows, :] = val
        elif ct < n_x_tiles + n_b_tiles:
            b_s[ct - n_x_tiles, rows, :] = val
        else:
            c_s[ct - n_x_tiles - n_b_tiles, rows, :] = val

    xbc = _dot(h, wxbc_ref[...])
    span = _conv_span(tl)
    for ct in range(n_tiles):
        cs = slice(ct * LANES, (ct + 1) * LANES)
        cbuf[ct, SSM_PAD:SSM_PAD + tl, :] = xbc[:, cs]
        for r0 in range(0, tl, span):
            _strided_depthwise(cbuf, put_xbc, ct, cw_ref, SSM_CONV_WIDTH, SSM_PAD - SSM_HIST, r0, span, _silu)
        new_hist = cbuf[ct, tl + SSM_PAD - SSM_HIST:tl + SSM_PAD, :]
        cbuf[ct, SSM_PAD - SSM_HIST:SSM_PAD, :] = new_hist
        hist_out_ref[:, cs] = new_hist

    a_row = -jnp.exp(alog_row_ref[...])
    a_col = -jnp.exp(alog_col_ref[...])
    dt = _softplus(_dot(h, wdt_ref[...]) + dtb_row_ref[...])
    acum = _dot(tri3_ref[...], jnp.concatenate(_bf16_pieces(dt * a_row), axis=0))
    spread_s[0] = jnp.concatenate(_bf16_pieces(dt), axis=-1)
    spread_s[1] = jnp.concatenate(_bf16_pieces(acum), axis=-1)
    dt_t = _softplus(_dot_nt(wdtt_ref[...], h) + dtb_col_ref[...])
    da_t = dt_t * a_col
    for c in range(n_blocks):
        tot = jnp.exp(jnp.sum(da_t[:, c * q:(c + 1) * q], axis=1, keepdims=True))
        tot = jnp.broadcast_to(tot, (SSM_HEADS, SSM_STATE))
        for g in range(SSM_GROUPS):
            et_s[c, g] = tot[g * SSM_HPG:(g + 1) * SSM_HPG, :]

    li = lax.broadcasted_iota(jnp.int32, (q, SSM_GROUP_DIM), 0)
    si = lax.broadcasted_iota(jnp.int32, (q, SSM_GROUP_DIM), 1) % slot
    diag = li == si
    causal = li >= si
    rr = lax.broadcasted_iota(jnp.int32, (SSM_GROUP_DIM, SSM_GROUP_DIM), 0) // slot
    rc = lax.broadcasted_iota(jnp.int32, (SSM_GROUP_DIM, SSM_GROUP_DIM), 1) // SSM_HEAD_DIM
    same_head = rr == rc

    def pad_rows(v):
        if q == slot:
            return v
        return jnp.concatenate([v, jnp.zeros((slot - q, v.shape[1]), v.dtype)], axis=0)

    def block(c, carry):
        rows = pl.ds(pl.multiple_of(c * q, q), q)
        ex = _dot(jnp.concatenate([spread_s[0, rows, :], spread_s[1, rows, :]], axis=0), e3_ref[...])
        for g in range(SSM_GROUPS):
            gs = slice(g * SSM_GROUP_DIM, (g + 1) * SSM_GROUP_DIM)
            de = ex[:q, gs]
            ae = ex[q:, gs]
            total = ae[q - 1:q, :]
            row_a = jnp.sum(jnp.where(diag, ae, 0.0), axis=0, keepdims=True)
            row_dt = jnp.sum(jnp.where(diag, de, 0.0), axis=0, keepdims=True)
            lmat = jnp.exp(jnp.where(causal, ae - row_a, NEG_BIG))
            cg = c_s[g, rows, :].astype(BF16)
            bg = b_s[g, rows, :].astype(BF16)
            cb = _dot_nt(cg, jnp.concatenate([pad_rows(bg)] * SSM_HPG, axis=0))
            m = (cb * lmat * row_dt).astype(BF16)
            xg = jnp.concatenate([xs_s[2 * g, rows, :], xs_s[2 * g + 1, rows, :]], axis=-1)
            xb = pad_rows(xg.astype(BF16))
            xdiag = jnp.where(same_head, jnp.concatenate([xb] * SSM_HPG, axis=0), jnp.zeros_like(xb[:1, :1]))
            st = state_ref[g]
            y = _dot(m, xdiag) + jnp.exp(ae) * _dot_nt(cg, st.astype(BF16)) + dskip_ref[g] * xg
            y_s[g, rows, :] = y
            xd = (xg * (de * jnp.exp(total - ae))).astype(BF16)
            et = et_s[c, g]
            decay = jnp.concatenate([jnp.broadcast_to(et[r:r + 1, :], (SSM_HEAD_DIM, SSM_STATE))
                                     for r in range(SSM_HPG)], axis=0)
            state_ref[g] = st * decay + _dot_tn(xd, bg)
        return carry

    lax.fori_loop(0, n_blocks, block, 0)

    z = _silu(_dot(h, wz_ref[...]))
    parts = []
    for g in range(SSM_GROUPS):
        yg = y_s[g] * z[:, g * SSM_GROUP_DIM:(g + 1) * SSM_GROUP_DIM]
        parts.append(yg * lax.rsqrt(jnp.mean(yg * yg, axis=-1, keepdims=True) + EPS))
    y = (jnp.concatenate(parts, axis=-1) * ng_ref[...]).astype(BF16)
    out_ssm = _dot(y, wso_ref[...])
    merged = p_ref[...] + _sigmoid(_dot(h, wgs_ref[...])) * out_ssm
    mix = _dot(merged.astype(BF16), wo_ref[...])
    xo_ref[...] = x + _rms(mix, gpost_ref[...])


def _ssd_constants(tl, q):
    i = jnp.arange(tl)
    tri = ((i[:, None] // q == i[None, :] // q) & (i[None, :] <= i[:, None])).astype(BF16)
    tri3 = jnp.concatenate([tri] * 3, axis=1)
    head_of_lane = jnp.arange(D_INNER) // SSM_HEAD_DIM
    e = (jnp.arange(SSM_HEADS)[:, None] == head_of_lane[None, :]).astype(BF16)
    e3 = jnp.concatenate([e] * 3, axis=0)
    return tri3, e3


def _ssd_merge_branch(x, p, hist, state, w, tl, q):
    bsz, seqlen, _ = x.shape
    assert seqlen % tl == 0 and tl % q == 0 and q <= SSM_HEAD_DIM and tl % (SUBLANES * CONV_STEP) == 0
    tile = pl.BlockSpec((None, tl, D_MODEL), lambda b, t: (b, t, 0))
    per_batch = lambda r, c: pl.BlockSpec((None, r, c), lambda b, t: (b, 0, 0))
    state_spec = pl.BlockSpec((None, SSM_GROUPS, SSM_GROUP_DIM, SSM_STATE), lambda b, t: (b, 0, 0, 0))
    tri3, e3 = _ssd_constants(tl, q)
    weights = [w['g_mix_pre'], w['w_z'], w['w_xbc'], w['w_dt'], w['w_dt_t'], w['w_gate_ssm'],
               w['ssm_taps'], w['dt_bias_row'], w['dt_bias_col'], w['a_log_row'],
               w['a_log_col'], w['d_skip_wide'], w['ssm_norm_g'], w['w_ssm_out'], w['w_o'], w['g_mix_post'],
               tri3, e3]
    return pl.pallas_call(
        functools.partial(_ssd_merge_kernel, q),
        grid=(bsz, seqlen // tl),
        in_specs=[tile, tile, per_batch(SSM_HIST, SSM_CONV_DIM), state_spec]
        + [_resident(a.shape) for a in weights],
        out_specs=[tile, per_batch(SSM_HIST, SSM_CONV_DIM), state_spec],
        out_shape=[jax.ShapeDtypeStruct(x.shape, F32),
                   jax.ShapeDtypeStruct((bsz, SSM_HIST, SSM_CONV_DIM), F32),
                   jax.ShapeDtypeStruct((bsz, SSM_GROUPS, SSM_GROUP_DIM, SSM_STATE), F32)],
        scratch_shapes=[pltpu.VMEM((SSM_CONV_DIM // LANES, SSM_PAD + tl, LANES), F32),
                        pltpu.VMEM((D_INNER // LANES, tl, LANES), F32),
                        pltpu.VMEM((SSM_GROUPS, tl, SSM_STATE), F32),
                        pltpu.VMEM((SSM_GROUPS, tl, SSM_STATE), F32),
                        pltpu.VMEM((SSM_GROUPS, tl, SSM_GROUP_DIM), F32),
                        pltpu.VMEM((tl // q, SSM_GROUPS, SSM_HPG, SSM_STATE), F32),
                        pltpu.VMEM((2, tl, 3 * SSM_HEADS), BF16)],
        compiler_params=_params(("arbitrary", "arbitrary")),
        name="ssd_merge_branch",
    )(x, p, hist, state, *weights)


def _ffn_kernel(x_ref, gpre_ref, wg_ref, wu_ref, wd_ref, gpost_ref, o_ref):
    x = x_ref[...]
    hn = _rms(x, gpre_ref[...]).astype(BF16)
    a = (_silu(_dot(hn, wg_ref[...])) * _dot(hn, wu_ref[...])).astype(BF16)
    o_ref[...] = x + _rms(_dot(a, wd_ref[...]), gpost_ref[...])


def _ffn(x2d, w, tm):
    rows = x2d.shape[0]
    assert rows % tm == 0
    tile = pl.BlockSpec((tm, D_MODEL), lambda i: (i, 0))
    weights = [w['g_ffn_pre'], w['w_gate'], w['w_up'], w['w_down'], w['g_ffn_post']]
    return pl.pallas_call(
        _ffn_kernel,
        grid=(rows // tm,),
        in_specs=[tile] + [_resident(a.shape) for a in weights],
        out_specs=tile,
        out_shape=jax.ShapeDtypeStruct(x2d.shape, F32),
        compiler_params=_params(("arbitrary",)),
        name="ffn",
    )(x2d, *weights)


def _layer(x, k_b, v_b, conv_hist, ssm_hist, ssm_state, w, tl, q):
    bsz, seqlen, _ = x.shape
    p, new_conv = _conv_mem_branch(x, conv_hist, k_b, v_b, w, tl)
    state = ssm_state.reshape(bsz, SSM_GROUPS, SSM_GROUP_DIM, SSM_STATE)
    x1, new_ssm_conv, new_state = _ssd_merge_branch(x, p, ssm_hist, state, w, tl, q)
    x2 = _ffn(x1.reshape(bsz * seqlen, D_MODEL), w, min(512, bsz * seqlen)).reshape(x.shape)
    return x2, new_conv, new_ssm_conv, new_state.reshape(bsz, SSM_HEADS, SSM_HEAD_DIM, SSM_STATE)


def _layer_weights(i, g_mix_pre, w_in, dw_w, dw_b, cln_g, cln_b, w_conv_out, ssm_conv_w, ssm_conv_b,
                   dt_bias, a_log, d_skip, ssm_norm_g, w_ssm_out, w_mem_out, w_o, g_mix_post, g_ffn_pre,
                   w_gate, w_up, w_down, g_ffn_post):
    row = lambda a: a[i].reshape(1, -1).astype(F32)
    col = lambda a: a[i].reshape(-1, 1).astype(F32)
    win = w_in[i].astype(BF16)
    return {
        'g_mix_pre': row(g_mix_pre),
        'w_glu': win[:, :N_GLU],
        'w_z': win[:, OFF_Z:OFF_XBC],
        'w_xbc': win[:, OFF_XBC:OFF_DT],
        'w_dt': win[:, OFF_DT:OFF_Q],
        'w_dt_t': win[:, OFF_DT:OFF_Q].T,
        'w_q': win[:, OFF_Q:OFF_GATE],
        'w_gate_conv': win[:, OFF_GATE:OFF_GATE + D_MODEL],
        'w_gate_ssm': win[:, OFF_GATE + D_MODEL:OFF_GATE + 2 * D_MODEL],
        'w_gate_mem': win[:, OFF_GATE + 2 * D_MODEL:],
        'dw_taps': _tap_table(dw_w[i], dw_b[i]), 'cln_g': row(cln_g), 'cln_b': row(cln_b),
        'w_conv_out': w_conv_out[i].astype(BF16),
        'ssm_taps': _tap_table(ssm_conv_w[i], ssm_conv_b[i]),
        'dt_bias_row': row(dt_bias), 'dt_bias_col': col(dt_bias),
        'a_log_row': row(a_log), 'a_log_col': col(a_log),
        'd_skip_wide': jnp.repeat(d_skip[i].astype(F32), SSM_HEAD_DIM).reshape(SSM_GROUPS, 1, SSM_GROUP_DIM),
        'ssm_norm_g': row(ssm_norm_g),
        'w_ssm_out': w_ssm_out[i].astype(BF16),
        'w_mem_out': w_mem_out[i].astype(BF16),
        'w_o': w_o[i].astype(BF16),
        'g_mix_post': row(g_mix_post), 'g_ffn_pre': row(g_ffn_pre),
        'w_gate': w_gate[i].astype(BF16), 'w_up': w_up[i].astype(BF16), 'w_down': w_down[i].astype(BF16),
        'g_ffn_post': row(g_ffn_post),
    }


def kernel(x_prompt, x_sample, mem_prompt, cache_mem_k, cache_mem_v, state_conv, state_ssm_conv, state_ssm, g_mix_pre, w_in, dw_w, dw_b, cln_g, cln_b, w_conv_out, ssm_conv_w, ssm_conv_b, dt_bias, a_log, d_skip, ssm_norm_g, w_ssm_out, mem_norm_g, w_mem_k, w_mem_v, w_mem_out, w_o, g_mix_post, g_ffn_pre, w_gate, w_up, w_down, g_ffn_post):
    depth = w_in.shape[0]
    bp, lp, _ = x_prompt.shape
    bs, ls, _ = x_sample.shape
    q_p = SSD_BLOCK if lp % SSD_BLOCK == 0 else lp
    q_s = SSD_BLOCK if ls % SSD_BLOCK == 0 else ls
    tl_p = 256 if lp % 256 == 0 else q_p
    tl_s = 256 if ls % 256 == 0 else q_s
    xp, xs = x_prompt, x_sample
    mem2d = mem_prompt.reshape(bp * N_MEM, D_MODEL)
    zc = jnp.zeros((bp, CONV_HIST, D_CONV), F32)
    zsc = jnp.zeros((bp, SSM_HIST, SSM_CONV_DIM), F32)
    zs = jnp.zeros((bp, SSM_HEADS, SSM_HEAD_DIM, SSM_STATE), F32)
    outs = [[] for _ in range(8)]
    for i in range(depth):
        w = _layer_weights(i, g_mix_pre, w_in, dw_w, dw_b, cln_g, cln_b, w_conv_out, ssm_conv_w, ssm_conv_b,
                           dt_bias, a_log, d_skip, ssm_norm_g, w_ssm_out, w_mem_out, w_o, g_mix_post,
                           g_ffn_pre, w_gate, w_up, w_down, g_ffn_post)
        w_kv = jnp.concatenate([w_mem_k[i], w_mem_v[i]], axis=1).astype(BF16)
        mk, mv, mk_b, mv_b = _memory_kv(mem2d, mem_norm_g[i].reshape(1, D_MODEL).astype(F32), w_kv)
        xp, c_new, sc_new, s_new = _layer(xp, mk_b.reshape(bp, N_MEM, D_MEM), mv_b.reshape(bp, N_MEM, D_MEM),
                                          zc, zsc, zs, w, tl_p, q_p)
        outs[0].append(mk.reshape(bp, N_MEM, MEM_HEADS, MEM_HEAD_DIM))
        outs[1].append(mv.reshape(bp, N_MEM, MEM_HEADS, MEM_HEAD_DIM))
        outs[2].append(c_new)
        outs[3].append(sc_new)
        outs[4].append(s_new)
        ck = cache_mem_k[i].reshape(bs, N_MEM, D_MEM).astype(BF16)
        cv = cache_mem_v[i].reshape(bs, N_MEM, D_MEM).astype(BF16)
        xs, c_new, sc_new, s_new = _layer(xs, ck, cv, state_conv[i], state_ssm_conv[i], state_ssm[i], w, tl_s, q_s)
        outs[5].append(c_new)
        outs[6].append(sc_new)
        outs[7].append(s_new)
    return (xp, xs) + tuple(jnp.stack(o) for o in outs)
```

```python
import functools

import jax
import jax.numpy as jnp
from jax import lax
from jax.experimental import pallas as pl
from jax.experimental.pallas import tpu as pltpu

F32 = jnp.float32
BF16 = jnp.bfloat16

EPS = 1e-6
D_MODEL = 1024
D_CONV = D_MODEL
CONV_WIDTH = 31
CONV_HIST = CONV_WIDTH - 1
D_INNER = 2 * D_MODEL
SSM_HEAD_DIM = 64
SSM_HEADS = D_INNER // SSM_HEAD_DIM
SSM_GROUPS = 8
SSM_HPG = SSM_HEADS // SSM_GROUPS
SSM_GROUP_DIM = SSM_HPG * SSM_HEAD_DIM
SSM_STATE = 128
SSM_CONV_WIDTH = 4
SSM_HIST = SSM_CONV_WIDTH - 1
SSM_CONV_DIM = D_INNER + 2 * SSM_GROUPS * SSM_STATE
SSD_BLOCK = 64
N_MEM = 256
MEM_HEADS = 4
MEM_HEAD_DIM = 128
D_MEM = MEM_HEADS * MEM_HEAD_DIM
D_FF = 2816
N_GLU = 2 * D_CONV
OFF_Z = N_GLU
OFF_XBC = OFF_Z + D_INNER
OFF_DT = OFF_XBC + SSM_CONV_DIM
OFF_Q = OFF_DT + SSM_HEADS
OFF_GATE = OFF_Q + D_MEM

SUBLANES = 8
LANES = 128
CONV_PAD = 32
SSM_PAD = 8
ROW_BLOCK = 32
CONV_STEP = 4
STAGE_EVERY = 4
TILE_ROWS = 256
SSD_SHORT_ROWS = 128
VMEM_LIMIT_BYTES = 56 * 1024 * 1024
LOG2_E = 1.4426950408889634
NEG_BIG = -1e30


def _dot(a, b):
    return jnp.dot(a, b, preferred_element_type=F32)


def _dot_nt(a, b):
    return lax.dot_general(a, b, (((1,), (1,)), ((), ())), preferred_element_type=F32)


def _dot_tn(a, b):
    return lax.dot_general(a, b, (((0,), (0,)), ((), ())), preferred_element_type=F32)


def _rms(x, g):
    return x * lax.rsqrt(jnp.mean(x * x, axis=-1, keepdims=True) + EPS) * g


def _sigmoid(x):
    return 1.0 / (1.0 + jnp.exp(-x))


def _silu(x):
    half = 0.5 * x
    return half + half * jnp.tanh(half)


def _softplus(x):
    return jnp.maximum(x, 0.0) + jnp.log1p(jnp.exp(-jnp.abs(x)))


def _bf16_pieces(x):
    hi = x.astype(BF16)
    r1 = x - hi.astype(F32)
    mid = r1.astype(BF16)
    lo = (r1 - mid.astype(F32)).astype(BF16)
    return [hi, mid, lo]


def _resident(shape):
    nd = len(shape)
    return pl.BlockSpec(shape, lambda *_: (0,) * nd, pipeline_mode=pl.Buffered(1))


def _params(sem):
    return pltpu.CompilerParams(dimension_semantics=sem, vmem_limit_bytes=VMEM_LIMIT_BYTES)


def _memkv_kernel(mem_ref, g_ref, w_ref, k_ref, v_ref, kb_ref, vb_ref):
    h = _rms(mem_ref[...], g_ref[...]).astype(BF16)
    kv = _dot(h, w_ref[...])
    k = kv[:, :D_MEM]
    v = kv[:, D_MEM:]
    k_ref[...] = k
    v_ref[...] = v
    kb_ref[...] = k.astype(BF16)
    vb_ref[...] = v.astype(BF16)


def _memory_kv(mem2d, g, w_kv):
    rows = mem2d.shape[0]
    tm = 512
    assert rows % tm == 0
    row_spec = pl.BlockSpec((tm, D_MODEL), lambda i: (i, 0))
    out_spec = pl.BlockSpec((tm, D_MEM), lambda i: (i, 0))
    return pl.pallas_call(
        _memkv_kernel,
        grid=(rows // tm,),
        in_specs=[row_spec, _resident((1, D_MODEL)), _resident((D_MODEL, 2 * D_MEM))],
        out_specs=[out_spec] * 4,
        out_shape=[jax.ShapeDtypeStruct((rows, D_MEM), F32)] * 2
        + [jax.ShapeDtypeStruct((rows, D_MEM), BF16)] * 2,
        compiler_params=_params(("arbitrary",)),
        name="memory_kv",
    )(mem2d, g, w_kv)


def _conv_span(tl):
    blk = SUBLANES * CONV_STEP
    return 2 * blk if tl % (2 * blk) == 0 else blk


def _strided_depthwise(load, put, w_ref, lane_tile, width, r0, span, finish):
    blk = SUBLANES * CONV_STEP
    starts = [s0 + j for s0 in range(r0, r0 + span, blk) for j in range(CONV_STEP)]
    accs = [w_ref[lane_tile, width]] * len(starts)
    for k in range(width):
        wk = w_ref[lane_tile, k]
        accs = [a + wk * load(s0 + k, CONV_STEP) for a, s0 in zip(accs, starts)]
    for a, s0 in zip(accs, starts):
        put(s0, CONV_STEP, finish(a))


def _strided_rows(start, stride):
    return pl.ds(start, SUBLANES, stride=stride)


def _tap_table(w, b):
    wb = jnp.concatenate([w, b[None, :]], axis=0).astype(F32)
    wb = wb.reshape(wb.shape[0], -1, 1, LANES).transpose(1, 0, 2, 3)
    return jnp.broadcast_to(wb, wb.shape[:2] + (SUBLANES, LANES))


def _conv_mem_kernel(x_ref, hist_ref, k_ref, v_ref, g_ref, wglu_ref, wgc_ref, wq_ref, wgm_ref,
                     dww_ref, clng_ref, clnb_ref, wco_ref, wmo_ref,
                     p_ref, hist_out_ref, ubuf, ybuf, act):
    nseq, tl = x_ref.shape[0], x_ref.shape[1]
    rows = nseq * tl
    t = pl.program_id(1)
    n_tiles = D_CONV // LANES
    h = _rms(x_ref[...].reshape(rows, D_MODEL), g_ref[...]).astype(BF16)

    glu = _dot(h, wglu_ref[...])
    u = glu[:, :D_CONV] * _sigmoid(glu[:, D_CONV:])

    @pl.when(t == 0)
    def _():
        for b in range(nseq):
            for ct in range(n_tiles):
                ubuf[b, ct, CONV_PAD - CONV_HIST:CONV_PAD, :] = hist_ref[b, :, ct * LANES:(ct + 1) * LANES]

    span = _conv_span(tl)
    for b in range(nseq):
        for ct in range(n_tiles):
            ubuf[b, ct, CONV_PAD:CONV_PAD + tl, :] = u[b * tl:(b + 1) * tl, ct * LANES:(ct + 1) * LANES]
        for r0 in range(0, tl, span):
            for ct in range(n_tiles):
                def load(start, stride, b=b, ct=ct):
                    return ubuf[b, ct, _strided_rows(CONV_PAD - CONV_HIST + start, stride), :]

                def put(start, stride, val, b=b, ct=ct):
                    ybuf[ct, _strided_rows(b * tl + start, stride), :] = val

                _strided_depthwise(load, put, dww_ref, ct, CONV_WIDTH, r0, span, lambda a: a)
            for r1 in range(b * tl + r0, b * tl + r0 + span, ROW_BLOCK):
                y = jnp.concatenate([ybuf[ct, r1:r1 + ROW_BLOCK, :] for ct in range(n_tiles)], axis=-1)
                mu = jnp.mean(y, axis=-1, keepdims=True)
                yc = y - mu
                var = jnp.mean(yc * yc, axis=-1, keepdims=True)
                yn = yc * lax.rsqrt(var + EPS) * clng_ref[...] + clnb_ref[...]
                act[r1:r1 + ROW_BLOCK, :] = _silu(yn).astype(BF16)
        for ct in range(n_tiles):
            new_hist = ubuf[b, ct, tl + CONV_PAD - CONV_HIST:tl + CONV_PAD, :]
            ubuf[b, ct, CONV_PAD - CONV_HIST:CONV_PAD, :] = new_hist
            hist_out_ref[b, :, ct * LANES:(ct + 1) * LANES] = new_hist

    out_conv = _dot(act[...], wco_ref[...])
    merged = _sigmoid(_dot(h, wgc_ref[...])) * out_conv

    q = _dot(h, wq_ref[...]).astype(BF16)
    per_seq = []
    for b in range(nseq):
        heads = []
        for hd in range(MEM_HEADS):
            sl = slice(hd * MEM_HEAD_DIM, (hd + 1) * MEM_HEAD_DIM)
            s = _dot_nt(q[b * tl:(b + 1) * tl, sl], k_ref[b, :, sl]) * (MEM_HEAD_DIM ** -0.5)
            e = jnp.exp(s - jnp.max(s, axis=-1, keepdims=True))
            heads.append(_dot(e.astype(BF16), v_ref[b, :, sl]) / jnp.sum(e, axis=-1, keepdims=True))
        per_seq.append(jnp.concatenate(heads, axis=-1))
    o = jnp.concatenate(per_seq, axis=0).astype(BF16)
    out_mem = _dot(o, wmo_ref[...])
    merged = merged + _sigmoid(_dot(h, wgm_ref[...])) * out_mem
    p_ref[...] = merged.reshape(nseq, tl, D_MODEL)


def _conv_mem_branch(x, hist, k_b, v_b, w, nseq, tl):
    bsz, seqlen, _ = x.shape
    rows = nseq * tl
    assert bsz % nseq == 0 and seqlen % tl == 0 and tl >= CONV_HIST
    assert tl % ROW_BLOCK == 0 and tl % (SUBLANES * CONV_STEP) == 0
    tile = pl.BlockSpec((nseq, tl, D_MODEL), lambda b, t: (b, t, 0))
    per_seq = lambda r, c: pl.BlockSpec((nseq, r, c), lambda b, t: (b, 0, 0))
    weights = [w['g_mix_pre'], w['w_glu'], w['w_gate_conv'], w['w_q'], w['w_gate_mem'],
               w['dw_taps'], w['cln_g'], w['cln_b'], w['w_conv_out'], w['w_mem_out']]
    return pl.pallas_call(
        _conv_mem_kernel,
        grid=(bsz // nseq, seqlen // tl),
        in_specs=[tile, per_seq(CONV_HIST, D_CONV), per_seq(N_MEM, D_MEM), per_seq(N_MEM, D_MEM)]
        + [_resident(a.shape) for a in weights],
        out_specs=[tile, per_seq(CONV_HIST, D_CONV)],
        out_shape=[jax.ShapeDtypeStruct(x.shape, F32),
                   jax.ShapeDtypeStruct((bsz, CONV_HIST, D_CONV), F32)],
        scratch_shapes=[pltpu.VMEM((nseq, D_CONV // LANES, CONV_PAD + tl, LANES), F32),
                        pltpu.VMEM((D_CONV // LANES, rows, LANES), F32),
                        pltpu.VMEM((rows, D_CONV), BF16)],
        compiler_params=_params(("arbitrary", "arbitrary")),
        name="conv_mem_branch",
    )(x, hist, k_b, v_b, *weights)


def _ssd_merge_kernel(q, x_ref, p_ref, hist_ref, state_in_ref, g_ref, wz_ref, wxbc_ref, wdt_ref, wdtt_ref,
                      wgs_ref, cw_ref, dtb_row_ref, dtb_col_ref, alog_row_ref, alog_col_ref,
                      dskip_ref, ng_ref, wso_ref, wo_ref, gpost_ref, tri3_ref, tri3t_ref, e3_ref,
                      xo_ref, hist_out_ref, state_ref, cbuf, xs_s, b_s, c_s, y_s, et_s, spread_s, src_s):
    nseq, tl = x_ref.shape[0], x_ref.shape[1]
    rows = nseq * tl
    t = pl.program_id(1)
    n_tiles = SSM_CONV_DIM // LANES
    n_x_tiles = D_INNER // LANES
    n_b_tiles = SSM_GROUPS * SSM_STATE // LANES
    n_blocks = rows // q
    blocks_per_seq = tl // q
    slot = SSM_HEAD_DIM
    x = x_ref[...].reshape(rows, D_MODEL)
    h = _rms(x, g_ref[...]).astype(BF16)

    v = {}

    def stage_proj():
        v['dt_raw'] = _dot(h, wdt_ref[...])
        v['dtt_raw'] = _dot_nt(wdtt_ref[...], h)

    def stage_softplus():
        a_row = -jnp.exp(alog_row_ref[...]) * LOG2_E
        a_col = -jnp.exp(alog_col_ref[...]) * LOG2_E
        v['dt'] = _softplus(v['dt_raw'] + dtb_row_ref[...])
        v['dt_t'] = _softplus(v['dtt_raw'] + dtb_col_ref[...])
        v['da'] = v['dt'] * a_row
        v['da_t'] = v['dt_t'] * a_col

    def stage_cumsum():
        v['acum'] = _dot(tri3_ref[...], jnp.concatenate(_bf16_pieces(v['da']), axis=0))
        v['acum_t'] = _dot(jnp.concatenate(_bf16_pieces(v['da_t']), axis=-1), tri3t_ref[...])

    def stage_spread():
        spread_s[0] = jnp.concatenate(_bf16_pieces(v['dt']), axis=-1)
        spread_s[1] = jnp.concatenate(_bf16_pieces(v['acum']), axis=-1)
        v['src_t'] = v['acum_t'] - jnp.log2(v['dt_t'])
        for c in range(n_blocks):
            tot = jnp.exp2(jnp.sum(v['da_t'][:, c * q:(c + 1) * q], axis=1, keepdims=True))
            tot = jnp.broadcast_to(tot, (SSM_HEADS, SSM_STATE))
            for g in range(SSM_GROUPS):
                et_s[c, g] = tot[g * SSM_HPG:(g + 1) * SSM_HPG, :]

    def stage_rows(c):
        def run():
            for g in range(SSM_GROUPS):
                pieces = []
                for r in range(SSM_HPG):
                    hh = g * SSM_HPG + r
                    pieces.append(v['src_t'][hh:hh + 1, c * q:(c + 1) * q])
                    if q < slot:
                        pieces.append(jnp.zeros((1, slot - q), F32))
                src_s[c, g] = jnp.concatenate(pieces, axis=-1)
        return run

    stages = [stage_proj, stage_softplus, stage_cumsum, stage_spread] + [stage_rows(c) for c in range(n_blocks)]

    @pl.when(t == 0)
    def _():
        for b in range(nseq):
            for ct in range(n_tiles):
                cbuf[b, ct, SSM_PAD - SSM_HIST:SSM_PAD, :] = hist_ref[b, :, ct * LANES:(ct + 1) * LANES]
        state_ref[...] = state_in_ref[...]

    def xbc_dest(ct):
        if ct < n_x_tiles:
            return xs_s, ct
        if ct < n_x_tiles + n_b_tiles:
            return b_s, ct - n_x_tiles
        return c_s, ct - n_x_tiles - n_b_tiles

    xbc = _dot(h, wxbc_ref[...])
    span = _conv_span(tl)
    for ct in range(n_tiles):
        cs = slice(ct * LANES, (ct + 1) * LANES)
        dst, di = xbc_dest(ct)
        for b in range(nseq):
            cbuf[b, ct, SSM_PAD:SSM_PAD + tl, :] = xbc[b * tl:(b + 1) * tl, cs]

            def load(start, stride, b=b):
                return cbuf[b, ct, _strided_rows(SSM_PAD - SSM_HIST + start, stride), :]

            def put(start, stride, val, b=b):
                dst[di, _strided_rows(b * tl + start, stride), :] = val

            for r0 in range(0, tl, span):
                _strided_depthwise(load, put, cw_ref, ct, SSM_CONV_WIDTH, r0, span, _silu)
            new_hist = cbuf[b, ct, tl + SSM_PAD - SSM_HIST:tl + SSM_PAD, :]
            cbuf[b, ct, SSM_PAD - SSM_HIST:SSM_PAD, :] = new_hist
            hist_out_ref[b, :, cs] = new_hist
        if ct % STAGE_EVERY == STAGE_EVERY - 1 and ct // STAGE_EVERY < len(stages):
            stages[ct // STAGE_EVERY]()
    for stage in stages[n_tiles // STAGE_EVERY:]:
        stage()

    li = lax.broadcasted_iota(jnp.int32, (q, SSM_GROUP_DIM), 0)
    si = lax.broadcasted_iota(jnp.int32, (q, SSM_GROUP_DIM), 1) % slot
    causal = li >= si
    rr = lax.broadcasted_iota(jnp.int32, (SSM_GROUP_DIM, SSM_GROUP_DIM), 0) // slot
    rc = lax.broadcasted_iota(jnp.int32, (SSM_GROUP_DIM, SSM_GROUP_DIM), 1) // SSM_HEAD_DIM
    same_head = rr == rc

    def pad_rows(a):
        if q == slot:
            return a
        return jnp.concatenate([a, jnp.zeros((slot - q, a.shape[1]), a.dtype)], axis=0)

    def block(c, carry):
        b = c // blocks_per_seq
        blk = pl.ds(pl.multiple_of(c * q, q), q)
        ex = _dot(jnp.concatenate([spread_s[0, blk, :], spread_s[1, blk, :]], axis=0), e3_ref[...])
        cgs = [c_s[g, blk, :].astype(BF16) for g in range(SSM_GROUPS)]
        bgs = [b_s[g, blk, :].astype(BF16) for g in range(SSM_GROUPS)]
        sts = [state_ref[b, g] for g in range(SSM_GROUPS)]
        cbs = [_dot_nt(cg, jnp.concatenate([pad_rows(bg)] * SSM_HPG, axis=0)) for cg, bg in zip(cgs, bgs)]
        yis = [_dot_nt(cg, st.astype(BF16)) for cg, st in zip(cgs, sts)]
        ms, yps, xds, xgs = [], [], [], []
        for g in range(SSM_GROUPS):
            gs = slice(g * SSM_GROUP_DIM, (g + 1) * SSM_GROUP_DIM)
            de = ex[:q, gs]
            ae = ex[q:, gs]
            total = ae[q - 1:q, :]
            lmat = jnp.exp2(jnp.where(causal, ae - src_s[c, g], NEG_BIG))
            ms.append((cbs[g] * lmat).astype(BF16))
            xg = jnp.concatenate([xs_s[2 * g, blk, :], xs_s[2 * g + 1, blk, :]], axis=-1)
            xgs.append(xg)
            yps.append(jnp.exp2(ae) * yis[g] + dskip_ref[g] * xg)
            xds.append((xg * (de * jnp.exp2(total - ae))).astype(BF16))
        for g in range(SSM_GROUPS):
            xb = pad_rows(xgs[g].astype(BF16))
            xdiag = jnp.where(same_head, jnp.concatenate([xb] * SSM_HPG, axis=0), jnp.zeros_like(xb[:1, :1]))
            y_s[g, blk, :] = _dot(ms[g], xdiag) + yps[g]
            et = et_s[c, g]
            decay = jnp.concatenate([jnp.broadcast_to(et[r:r + 1, :], (SSM_HEAD_DIM, SSM_STATE))
                                     for r in range(SSM_HPG)], axis=0)
            state_ref[b, g] = sts[g] * decay + _dot_tn(xds[g], bgs[g])
        return carry

    lax.fori_loop(0, n_blocks, block, 0)

    z = _silu(_dot(h, wz_ref[...]))
    parts = []
    for g in range(SSM_GROUPS):
        yg = y_s[g] * z[:, g * SSM_GROUP_DIM:(g + 1) * SSM_GROUP_DIM]
        parts.append(yg * lax.rsqrt(jnp.mean(yg * yg, axis=-1, keepdims=True) + EPS))
    y = (jnp.concatenate(parts, axis=-1) * ng_ref[...]).astype(BF16)
    out_ssm = _dot(y, wso_ref[...])
    merged = p_ref[...].reshape(rows, D_MODEL) + _sigmoid(_dot(h, wgs_ref[...])) * out_ssm
    mix = _dot(merged.astype(BF16), wo_ref[...])
    xo_ref[...] = (x + _rms(mix, gpost_ref[...])).reshape(nseq, tl, D_MODEL)


def _ssd_constants(rows, q):
    i = jnp.arange(rows)
    tri = ((i[:, None] // q == i[None, :] // q) & (i[None, :] <= i[:, None])).astype(BF16)
    tri3 = jnp.concatenate([tri] * 3, axis=1)
    tri3t = jnp.concatenate([tri.T] * 3, axis=0)
    head_of_lane = jnp.arange(D_INNER) // SSM_HEAD_DIM
    e = (jnp.arange(SSM_HEADS)[:, None] == head_of_lane[None, :]).astype(BF16)
    e3 = jnp.concatenate([e] * 3, axis=0)
    return tri3, tri3t, e3


def _ssd_merge_branch(x, p, hist, state, w, nseq, tl, q):
    bsz, seqlen, _ = x.shape
    rows = nseq * tl
    assert bsz % nseq == 0 and seqlen % tl == 0 and tl % q == 0 and q <= SSM_HEAD_DIM
    assert tl % (SUBLANES * CONV_STEP) == 0
    tile = pl.BlockSpec((nseq, tl, D_MODEL), lambda b, t: (b, t, 0))
    per_seq = lambda r, c: pl.BlockSpec((nseq, r, c), lambda b, t: (b, 0, 0))
    state_spec = pl.BlockSpec((nseq, SSM_GROUPS, SSM_GROUP_DIM, SSM_STATE), lambda b, t: (b, 0, 0, 0))
    tri3, tri3t, e3 = _ssd_constants(rows, q)
    weights = [w['g_mix_pre'], w['w_z'], w['w_xbc'], w['w_dt'], w['w_dt_t'], w['w_gate_ssm'],
               w['ssm_taps'], w['dt_bias_row'], w['dt_bias_col'], w['a_log_row'],
               w['a_log_col'], w['d_skip_wide'], w['ssm_norm_g'], w['w_ssm_out'], w['w_o'], w['g_mix_post'],
               tri3, tri3t, e3]
    return pl.pallas_call(
        functools.partial(_ssd_merge_kernel, q),
        grid=(bsz // nseq, seqlen // tl),
        in_specs=[tile, tile, per_seq(SSM_HIST, SSM_CONV_DIM), state_spec]
        + [_resident(a.shape) for a in weights],
        out_specs=[tile, per_seq(SSM_HIST, SSM_CONV_DIM), state_spec],
        out_shape=[jax.ShapeDtypeStruct(x.shape, F32),
                   jax.ShapeDtypeStruct((bsz, SSM_HIST, SSM_CONV_DIM), F32),
                   jax.ShapeDtypeStruct((bsz, SSM_GROUPS, SSM_GROUP_DIM, SSM_STATE), F32)],
        scratch_shapes=[pltpu.VMEM((nseq, SSM_CONV_DIM // LANES, SSM_PAD + tl, LANES), F32),
                        pltpu.VMEM((D_INNER // LANES, rows, LANES), F32),
                        pltpu.VMEM((SSM_GROUPS, rows, SSM_STATE), F32),
                        pltpu.VMEM((SSM_GROUPS, rows, SSM_STATE), F32),
                        pltpu.VMEM((SSM_GROUPS, rows, SSM_GROUP_DIM), F32),
                        pltpu.VMEM((rows // q, SSM_GROUPS, SSM_HPG, SSM_STATE), F32),
                        pltpu.VMEM((2, rows, 3 * SSM_HEADS), BF16),
                        pltpu.VMEM((rows // q, SSM_GROUPS, 1, SSM_GROUP_DIM), F32)],
        compiler_params=_params(("arbitrary", "arbitrary")),
        name="ssd_merge_branch",
    )(x, p, hist, state, *weights)


def _ffn_kernel(x_ref, gpre_ref, wg_ref, wu_ref, wd_ref, gpost_ref, o_ref):
    x = x_ref[...]
    hn = _rms(x, gpre_ref[...]).astype(BF16)
    a = (_silu(_dot(hn, wg_ref[...])) * _dot(hn, wu_ref[...])).astype(BF16)
    o_ref[...] = x + _rms(_dot(a, wd_ref[...]), gpost_ref[...])


def _ffn(x2d, w, tm):
    rows = x2d.shape[0]
    assert rows % tm == 0
    tile = pl.BlockSpec((tm, D_MODEL), lambda i: (i, 0))
    weights = [w['g_ffn_pre'], w['w_gate'], w['w_up'], w['w_down'], w['g_ffn_post']]
    return pl.pallas_call(
        _ffn_kernel,
        grid=(rows // tm,),
        in_specs=[tile] + [_resident(a.shape) for a in weights],
        out_specs=tile,
        out_shape=jax.ShapeDtypeStruct(x2d.shape, F32),
        compiler_params=_params(("arbitrary",)),
        name="ffn",
    )(x2d, *weights)


def _tile_rows(bsz, seqlen, short_rows):
    if seqlen % TILE_ROWS == 0:
        return TILE_ROWS, 1
    nseq = max(short_rows // seqlen, 1)
    while bsz % nseq:
        nseq -= 1
    return seqlen, nseq


def _layer(x, k_b, v_b, conv_hist, ssm_hist, ssm_state, w, q):
    bsz, seqlen, _ = x.shape
    tl, nseq = _tile_rows(bsz, seqlen, TILE_ROWS)
    p, new_conv = _conv_mem_branch(x, conv_hist, k_b, v_b, w, nseq, tl)
    tl, nseq = _tile_rows(bsz, seqlen, SSD_SHORT_ROWS)
    state = ssm_state.reshape(bsz, SSM_GROUPS, SSM_GROUP_DIM, SSM_STATE)
    x1, new_ssm_conv, new_state = _ssd_merge_branch(x, p, ssm_hist, state, w, nseq, tl, q)
    x2 = _ffn(x1.reshape(bsz * seqlen, D_MODEL), w, min(512, bsz * seqlen)).reshape(x.shape)
    return x2, new_conv, new_ssm_conv, new_state.reshape(bsz, SSM_HEADS, SSM_HEAD_DIM, SSM_STATE)


def _layer_weights(i, g_mix_pre, w_in, dw_w, dw_b, cln_g, cln_b, w_conv_out, ssm_conv_w, ssm_conv_b,
                   dt_bias, a_log, d_skip, ssm_norm_g, w_ssm_out, w_mem_out, w_o, g_mix_post, g_ffn_pre,
                   w_gate, w_up, w_down, g_ffn_post):
    row = lambda a: a[i].reshape(1, -1).astype(F32)
    col = lambda a: a[i].reshape(-1, 1).astype(F32)
    win = lambda a, b: w_in[i, :, a:b].astype(BF16)
    return {
        'g_mix_pre': row(g_mix_pre),
        'w_glu': win(0, N_GLU),
        'w_z': win(OFF_Z, OFF_XBC),
        'w_xbc': win(OFF_XBC, OFF_DT),
        'w_dt': win(OFF_DT, OFF_Q),
        'w_dt_t': win(OFF_DT, OFF_Q).T,
        'w_q': win(OFF_Q, OFF_GATE),
        'w_gate_conv': win(OFF_GATE, OFF_GATE + D_MODEL),
        'w_gate_ssm': win(OFF_GATE + D_MODEL, OFF_GATE + 2 * D_MODEL),
        'w_gate_mem': win(OFF_GATE + 2 * D_MODEL, OFF_GATE + 3 * D_MODEL),
        'dw_taps': _tap_table(dw_w[i], dw_b[i]), 'cln_g': row(cln_g), 'cln_b': row(cln_b),
        'w_conv_out': w_conv_out[i].astype(BF16),
        'ssm_taps': _tap_table(ssm_conv_w[i], ssm_conv_b[i]),
        'dt_bias_row': row(dt_bias), 'dt_bias_col': col(dt_bias),
        'a_log_row': row(a_log), 'a_log_col': col(a_log),
        'd_skip_wide': jnp.repeat(d_skip[i].astype(F32), SSM_HEAD_DIM).reshape(SSM_GROUPS, 1, SSM_GROUP_DIM),
        'ssm_norm_g': row(ssm_norm_g),
        'w_ssm_out': w_ssm_out[i].astype(BF16),
        'w_mem_out': w_mem_out[i].astype(BF16),
        'w_o': w_o[i].astype(BF16),
        'g_mix_post': row(g_mix_post), 'g_ffn_pre': row(g_ffn_pre),
        'w_gate': w_gate[i].astype(BF16), 'w_up': w_up[i].astype(BF16), 'w_down': w_down[i].astype(BF16),
        'g_ffn_post': row(g_ffn_post),
    }


def kernel(x_prompt, x_sample, mem_prompt, cache_mem_k, cache_mem_v, state_conv, state_ssm_conv, state_ssm, g_mix_pre, w_in, dw_w, dw_b, cln_g, cln_b, w_conv_out, ssm_conv_w, ssm_conv_b, dt_bias, a_log, d_skip, ssm_norm_g, w_ssm_out, mem_norm_g, w_mem_k, w_mem_v, w_mem_out, w_o, g_mix_post, g_ffn_pre, w_gate, w_up, w_down, g_ffn_post):
    depth = w_in.shape[0]
    bp, lp, _ = x_prompt.shape
    bs, ls, _ = x_sample.shape
    q_p = SSD_BLOCK if lp % SSD_BLOCK == 0 else lp
    q_s = SSD_BLOCK if ls % SSD_BLOCK == 0 else ls
    xp, xs = x_prompt, x_sample
    mem2d = mem_prompt.reshape(bp * N_MEM, D_MODEL)
    zc = jnp.zeros((bp, CONV_HIST, D_CONV), F32)
    zsc = jnp.zeros((bp, SSM_HIST, SSM_CONV_DIM), F32)
    zs = jnp.zeros((bp, SSM_HEADS, SSM_HEAD_DIM, SSM_STATE), F32)
    outs = [[] for _ in range(8)]
    for i in range(depth):
        w = _layer_weights(i, g_mix_pre, w_in, dw_w, dw_b, cln_g, cln_b, w_conv_out, ssm_conv_w, ssm_conv_b,
                           dt_bias, a_log, d_skip, ssm_norm_g, w_ssm_out, w_mem_out, w_o, g_mix_post,
                           g_ffn_pre, w_gate, w_up, w_down, g_ffn_post)
        w_kv = jnp.concatenate([w_mem_k[i], w_mem_v[i]], axis=1).astype(BF16)
        mk, mv, mk_b, mv_b = _memory_kv(mem2d, mem_norm_g[i].reshape(1, D_MODEL).astype(F32), w_kv)
        xp, c_new, sc_new, s_new = _layer(xp, mk_b.reshape(bp, N_MEM, D_MEM), mv_b.reshape(bp, N_MEM, D_MEM),
                                          zc, zsc, zs, w, q_p)
        outs[0].append(mk.reshape(bp, N_MEM, MEM_HEADS, MEM_HEAD_DIM))
        outs[1].append(mv.reshape(bp, N_MEM, MEM_HEADS, MEM_HEAD_DIM))
        outs[2].append(c_new)
        outs[3].append(sc_new)
        outs[4].append(s_new)
        ck = cache_mem_k[i].reshape(bs, N_MEM, D_MEM).astype(BF16)
        cv = cache_mem_v[i].reshape(bs, N_MEM, D_MEM).astype(BF16)
        xs, c_new, sc_new, s_new = _layer(xs, ck, cv, state_conv[i], state_ssm_conv[i], state_ssm[i], w, q_s)
        outs[5].append(c_new)
        outs[6].append(sc_new)
        outs[7].append(s_new)
    return (xp, xs) + tuple(jnp.stack(o) for o in outs)
```

```python
import functools

import jax
import jax.numpy as jnp
from jax import lax
from jax.experimental import pallas as pl
from jax.experimental.pallas import tpu as pltpu

F32 = jnp.float32
BF16 = jnp.bfloat16

EPS = 1e-6
D_MODEL = 1024
D_CONV = D_MODEL
CONV_WIDTH = 31
CONV_HIST = CONV_WIDTH - 1
D_INNER = 2 * D_MODEL
SSM_HEAD_DIM = 64
SSM_HEADS = D_INNER // SSM_HEAD_DIM
SSM_GROUPS = 8
SSM_HPG = SSM_HEADS // SSM_GROUPS
SSM_GROUP_DIM = SSM_HPG * SSM_HEAD_DIM
SSM_STATE = 128
SSM_CONV_WIDTH = 4
SSM_HIST = SSM_CONV_WIDTH - 1
SSM_CONV_DIM = D_INNER + 2 * SSM_GROUPS * SSM_STATE
SSD_BLOCK = 64
N_MEM = 256
MEM_HEADS = 4
MEM_HEAD_DIM = 128
D_MEM = MEM_HEADS * MEM_HEAD_DIM
D_FF = 2816
N_GLU = 2 * D_CONV
OFF_Z = N_GLU
OFF_XBC = OFF_Z + D_INNER
OFF_DT = OFF_XBC + SSM_CONV_DIM
OFF_Q = OFF_DT + SSM_HEADS
OFF_GATE = OFF_Q + D_MEM

SUBLANES = 8
LANES = 128
CONV_PAD = 32
SSM_PAD = 8
ROW_BLOCK = 32
CONV_STEP = 4
STAGE_EVERY = 4
CONV_TILE_ROWS = 512
SSD_TILE_ROWS = 256
CONV_SHORT_ROWS = 256
SSD_SHORT_ROWS = 128
VMEM_LIMIT_BYTES = 56 * 1024 * 1024
LOG2_E = 1.4426950408889634
NEG_BIG = -1e30


def _dot(a, b):
    return jnp.dot(a, b, preferred_element_type=F32)


def _dot_nt(a, b):
    return lax.dot_general(a, b, (((1,), (1,)), ((), ())), preferred_element_type=F32)


def _dot_tn(a, b):
    return lax.dot_general(a, b, (((0,), (0,)), ((), ())), preferred_element_type=F32)


def _rms(x, g):
    return x * lax.rsqrt(jnp.mean(x * x, axis=-1, keepdims=True) + EPS) * g


def _sigmoid(x):
    return 1.0 / (1.0 + jnp.exp(-x))


def _silu(x):
    half = 0.5 * x
    return half + half * jnp.tanh(half)


def _softplus(x):
    return jnp.maximum(x, 0.0) + jnp.log1p(jnp.exp(-jnp.abs(x)))


def _bf16_pieces(x):
    hi = x.astype(BF16)
    r1 = x - hi.astype(F32)
    mid = r1.astype(BF16)
    lo = (r1 - mid.astype(F32)).astype(BF16)
    return [hi, mid, lo]


def _resident(shape):
    nd = len(shape)
    return pl.BlockSpec(shape, lambda *_: (0,) * nd, pipeline_mode=pl.Buffered(1))


def _params(sem):
    return pltpu.CompilerParams(dimension_semantics=sem, vmem_limit_bytes=VMEM_LIMIT_BYTES)


def _memkv_kernel(mem_ref, g_ref, w_ref, k_ref, v_ref, kb_ref, vb_ref):
    h = _rms(mem_ref[...], g_ref[...]).astype(BF16)
    kv = _dot(h, w_ref[...])
    k = kv[:, :D_MEM]
    v = kv[:, D_MEM:]
    k_ref[...] = k
    v_ref[...] = v
    kb_ref[...] = k.astype(BF16)
    vb_ref[...] = v.astype(BF16)


def _memory_kv(mem2d, g, w_kv):
    rows = mem2d.shape[0]
    tm = 512
    assert rows % tm == 0
    row_spec = pl.BlockSpec((tm, D_MODEL), lambda i: (i, 0))
    out_spec = pl.BlockSpec((tm, D_MEM), lambda i: (i, 0))
    return pl.pallas_call(
        _memkv_kernel,
        grid=(rows // tm,),
        in_specs=[row_spec, _resident((1, D_MODEL)), _resident((D_MODEL, 2 * D_MEM))],
        out_specs=[out_spec] * 4,
        out_shape=[jax.ShapeDtypeStruct((rows, D_MEM), F32)] * 2
        + [jax.ShapeDtypeStruct((rows, D_MEM), BF16)] * 2,
        compiler_params=_params(("arbitrary",)),
        name="memory_kv",
    )(mem2d, g, w_kv)


def _conv_span(tl):
    blk = SUBLANES * CONV_STEP
    return 2 * blk if tl % (2 * blk) == 0 else blk


def _strided_depthwise(load, put, w_ref, lane_tile, width, r0, span, finish):
    blk = SUBLANES * CONV_STEP
    starts = [s0 + j for s0 in range(r0, r0 + span, blk) for j in range(CONV_STEP)]
    accs = [w_ref[lane_tile, width]] * len(starts)
    for k in range(width):
        wk = w_ref[lane_tile, k]
        accs = [a + wk * load(s0 + k, CONV_STEP) for a, s0 in zip(accs, starts)]
    for a, s0 in zip(accs, starts):
        put(s0, CONV_STEP, finish(a))


def _strided_rows(start, stride):
    return pl.ds(start, SUBLANES, stride=stride)


def _tap_table(w, b):
    wb = jnp.concatenate([w, b[None, :]], axis=0).astype(F32)
    wb = wb.reshape(wb.shape[0], -1, 1, LANES).transpose(1, 0, 2, 3)
    return jnp.broadcast_to(wb, wb.shape[:2] + (SUBLANES, LANES))


def _conv_mem_kernel(x_ref, hist_ref, k_ref, v_ref, g_ref, wglu_ref, wgc_ref, wq_ref, wgm_ref,
                     dww_ref, clng_ref, clnb_ref, wco_ref, wmo_ref,
                     p_ref, hist_out_ref, ubuf, ybuf, act):
    nseq, tl = x_ref.shape[0], x_ref.shape[1]
    rows = nseq * tl
    t = pl.program_id(1)
    n_tiles = D_CONV // LANES
    h = _rms(x_ref[...].reshape(rows, D_MODEL), g_ref[...]).astype(BF16)

    glu = _dot(h, wglu_ref[...])
    u = glu[:, :D_CONV] * _sigmoid(glu[:, D_CONV:])

    @pl.when(t == 0)
    def _():
        for b in range(nseq):
            for ct in range(n_tiles):
                ubuf[b, ct, CONV_PAD - CONV_HIST:CONV_PAD, :] = hist_ref[b, :, ct * LANES:(ct + 1) * LANES]

    span = _conv_span(tl)
    for b in range(nseq):
        for ct in range(n_tiles):
            ubuf[b, ct, CONV_PAD:CONV_PAD + tl, :] = u[b * tl:(b + 1) * tl, ct * LANES:(ct + 1) * LANES]
        for r0 in range(0, tl, span):
            for ct in range(n_tiles):
                def load(start, stride, b=b, ct=ct):
                    return ubuf[b, ct, _strided_rows(CONV_PAD - CONV_HIST + start, stride), :]

                def put(start, stride, val, b=b, ct=ct):
                    ybuf[ct, _strided_rows(b * tl + start, stride), :] = val

                _strided_depthwise(load, put, dww_ref, ct, CONV_WIDTH, r0, span, lambda a: a)
            for r1 in range(b * tl + r0, b * tl + r0 + span, ROW_BLOCK):
                y = jnp.concatenate([ybuf[ct, r1:r1 + ROW_BLOCK, :] for ct in range(n_tiles)], axis=-1)
                mu = jnp.mean(y, axis=-1, keepdims=True)
                yc = y - mu
                var = jnp.mean(yc * yc, axis=-1, keepdims=True)
                yn = yc * lax.rsqrt(var + EPS) * clng_ref[...] + clnb_ref[...]
                act[r1:r1 + ROW_BLOCK, :] = _silu(yn).astype(BF16)
        for ct in range(n_tiles):
            new_hist = ubuf[b, ct, tl + CONV_PAD - CONV_HIST:tl + CONV_PAD, :]
            ubuf[b, ct, CONV_PAD - CONV_HIST:CONV_PAD, :] = new_hist
            hist_out_ref[b, :, ct * LANES:(ct + 1) * LANES] = new_hist

    out_conv = _dot(act[...], wco_ref[...])
    merged = _sigmoid(_dot(h, wgc_ref[...])) * out_conv

    q = _dot(h, wq_ref[...]).astype(BF16)
    per_seq = []
    for b in range(nseq):
        heads = []
        for hd in range(MEM_HEADS):
            sl = slice(hd * MEM_HEAD_DIM, (hd + 1) * MEM_HEAD_DIM)
            s = _dot_nt(q[b * tl:(b + 1) * tl, sl], k_ref[b, :, sl]) * (MEM_HEAD_DIM ** -0.5)
            e = jnp.exp(s - jnp.max(s, axis=-1, keepdims=True))
            heads.append(_dot(e.astype(BF16), v_ref[b, :, sl]) / jnp.sum(e, axis=-1, keepdims=True))
        per_seq.append(jnp.concatenate(heads, axis=-1))
    o = jnp.concatenate(per_seq, axis=0).astype(BF16)
    out_mem = _dot(o, wmo_ref[...])
    merged = merged + _sigmoid(_dot(h, wgm_ref[...])) * out_mem
    p_ref[...] = merged.reshape(nseq, tl, D_MODEL)


def _conv_mem_branch(x, hist, k_b, v_b, w, nseq, tl):
    bsz, seqlen, _ = x.shape
    rows = nseq * tl
    assert bsz % nseq == 0 and seqlen % tl == 0 and tl >= CONV_HIST
    assert tl % ROW_BLOCK == 0 and tl % (SUBLANES * CONV_STEP) == 0
    tile = pl.BlockSpec((nseq, tl, D_MODEL), lambda b, t: (b, t, 0))
    per_seq = lambda r, c: pl.BlockSpec((nseq, r, c), lambda b, t: (b, 0, 0))
    weights = [w['g_mix_pre'], w['w_glu'], w['w_gate_conv'], w['w_q'], w['w_gate_mem'],
               w['dw_taps'], w['cln_g'], w['cln_b'], w['w_conv_out'], w['w_mem_out']]
    return pl.pallas_call(
        _conv_mem_kernel,
        grid=(bsz // nseq, seqlen // tl),
        in_specs=[tile, per_seq(CONV_HIST, D_CONV), per_seq(N_MEM, D_MEM), per_seq(N_MEM, D_MEM)]
        + [_resident(a.shape) for a in weights],
        out_specs=[tile, per_seq(CONV_HIST, D_CONV)],
        out_shape=[jax.ShapeDtypeStruct(x.shape, F32),
                   jax.ShapeDtypeStruct((bsz, CONV_HIST, D_CONV), F32)],
        scratch_shapes=[pltpu.VMEM((nseq, D_CONV // LANES, CONV_PAD + tl, LANES), F32),
                        pltpu.VMEM((D_CONV // LANES, rows, LANES), F32),
                        pltpu.VMEM((rows, D_CONV), BF16)],
        compiler_params=_params(("arbitrary", "arbitrary")),
        name="conv_mem_branch",
    )(x, hist, k_b, v_b, *weights)


def _ssd_merge_kernel(q, x_ref, p_ref, hist_ref, state_in_ref, g_ref, wz_ref, wxbc_ref, wdt_ref, wdtt_ref,
                      wgs_ref, cw_ref, dtb_row_ref, dtb_col_ref, alog_row_ref, alog_col_ref,
                      dskip_ref, ng_ref, wso_ref, wo_ref, gpost_ref, tri3_ref, tri3t_ref, e3_ref,
                      xo_ref, hist_out_ref, state_ref, cbuf, xs_s, b_s, c_s, y_s, et_s, spread_s, src_s):
    nseq, tl = x_ref.shape[0], x_ref.shape[1]
    rows = nseq * tl
    t = pl.program_id(1)
    n_tiles = SSM_CONV_DIM // LANES
    n_x_tiles = D_INNER // LANES
    n_b_tiles = SSM_GROUPS * SSM_STATE // LANES
    n_blocks = rows // q
    blocks_per_seq = tl // q
    slot = SSM_HEAD_DIM
    x = x_ref[...].reshape(rows, D_MODEL)
    h = _rms(x, g_ref[...]).astype(BF16)

    v = {}

    def stage_proj():
        v['dt_raw'] = _dot(h, wdt_ref[...])
        v['dtt_raw'] = _dot_nt(wdtt_ref[...], h)

    def stage_softplus():
        a_row = -jnp.exp(alog_row_ref[...]) * LOG2_E
        a_col = -jnp.exp(alog_col_ref[...]) * LOG2_E
        v['dt'] = _softplus(v['dt_raw'] + dtb_row_ref[...])
        v['dt_t'] = _softplus(v['dtt_raw'] + dtb_col_ref[...])
        v['da'] = v['dt'] * a_row
        v['da_t'] = v['dt_t'] * a_col

    def stage_cumsum():
        v['acum'] = _dot(tri3_ref[...], jnp.concatenate(_bf16_pieces(v['da']), axis=0))
        v['acum_t'] = _dot(jnp.concatenate(_bf16_pieces(v['da_t']), axis=-1), tri3t_ref[...])

    def stage_spread():
        spread_s[0] = jnp.concatenate(_bf16_pieces(v['dt']), axis=-1)
        spread_s[1] = jnp.concatenate(_bf16_pieces(v['acum']), axis=-1)
        v['src_t'] = v['acum_t'] - jnp.log2(v['dt_t'])
        for c in range(n_blocks):
            tot = jnp.exp2(jnp.sum(v['da_t'][:, c * q:(c + 1) * q], axis=1, keepdims=True))
            tot = jnp.broadcast_to(tot, (SSM_HEADS, SSM_STATE))
            for g in range(SSM_GROUPS):
                et_s[c, g] = tot[g * SSM_HPG:(g + 1) * SSM_HPG, :]

    def stage_rows(c):
        def run():
            for g in range(SSM_GROUPS):
                pieces = []
                for r in range(SSM_HPG):
                    hh = g * SSM_HPG + r
                    pieces.append(v['src_t'][hh:hh + 1, c * q:(c + 1) * q])
                    if q < slot:
                        pieces.append(jnp.zeros((1, slot - q), F32))
                src_s[c, g] = jnp.concatenate(pieces, axis=-1)
        return run

    stages = [stage_proj, stage_softplus, stage_cumsum, stage_spread] + [stage_rows(c) for c in range(n_blocks)]

    @pl.when(t == 0)
    def _():
        for b in range(nseq):
            for ct in range(n_tiles):
                cbuf[b, ct, SSM_PAD - SSM_HIST:SSM_PAD, :] = hist_ref[b, :, ct * LANES:(ct + 1) * LANES]
        state_ref[...] = state_in_ref[...]

    def xbc_dest(ct):
        if ct < n_x_tiles:
            return xs_s, ct
        if ct < n_x_tiles + n_b_tiles:
            return b_s, ct - n_x_tiles
        return c_s, ct - n_x_tiles - n_b_tiles

    xbc = _dot(h, wxbc_ref[...])
    span = _conv_span(tl)
    for ct in range(n_tiles):
        cs = slice(ct * LANES, (ct + 1) * LANES)
        dst, di = xbc_dest(ct)
        for b in range(nseq):
            cbuf[b, ct, SSM_PAD:SSM_PAD + tl, :] = xbc[b * tl:(b + 1) * tl, cs]

            def load(start, stride, b=b):
                return cbuf[b, ct, _strided_rows(SSM_PAD - SSM_HIST + start, stride), :]

            def put(start, stride, val, b=b):
                dst[di, _strided_rows(b * tl + start, stride), :] = val

            for r0 in range(0, tl, span):
                _strided_depthwise(load, put, cw_ref, ct, SSM_CONV_WIDTH, r0, span, _silu)
            new_hist = cbuf[b, ct, tl + SSM_PAD - SSM_HIST:tl + SSM_PAD, :]
            cbuf[b, ct, SSM_PAD - SSM_HIST:SSM_PAD, :] = new_hist
            hist_out_ref[b, :, cs] = new_hist
        if ct % STAGE_EVERY == STAGE_EVERY - 1 and ct // STAGE_EVERY < len(stages):
            stages[ct // STAGE_EVERY]()
    for stage in stages[n_tiles // STAGE_EVERY:]:
        stage()

    li = lax.broadcasted_iota(jnp.int32, (q, SSM_GROUP_DIM), 0)
    si = lax.broadcasted_iota(jnp.int32, (q, SSM_GROUP_DIM), 1) % slot
    causal = li >= si
    rr = lax.broadcasted_iota(jnp.int32, (SSM_GROUP_DIM, SSM_GROUP_DIM), 0) // slot
    rc = lax.broadcasted_iota(jnp.int32, (SSM_GROUP_DIM, SSM_GROUP_DIM), 1) // SSM_HEAD_DIM
    same_head = rr == rc

    def pad_rows(a):
        if q == slot:
            return a
        return jnp.concatenate([a, jnp.zeros((slot - q, a.shape[1]), a.dtype)], axis=0)

    def block(c, carry):
        b = c // blocks_per_seq
        blk = pl.ds(pl.multiple_of(c * q, q), q)
        ex = _dot(jnp.concatenate([spread_s[0, blk, :], spread_s[1, blk, :]], axis=0), e3_ref[...])
        cgs = [c_s[g, blk, :].astype(BF16) for g in range(SSM_GROUPS)]
        bgs = [b_s[g, blk, :].astype(BF16) for g in range(SSM_GROUPS)]
        sts = [state_ref[b, g] for g in range(SSM_GROUPS)]
        cbs = [_dot_nt(cg, jnp.concatenate([pad_rows(bg)] * SSM_HPG, axis=0)) for cg, bg in zip(cgs, bgs)]
        yis = [_dot_nt(cg, st.astype(BF16)) for cg, st in zip(cgs, sts)]
        ms, yps, xds, xgs = [], [], [], []
        for g in range(SSM_GROUPS):
            gs = slice(g * SSM_GROUP_DIM, (g + 1) * SSM_GROUP_DIM)
            de = ex[:q, gs]
            ae = ex[q:, gs]
            total = ae[q - 1:q, :]
            lmat = jnp.exp2(jnp.where(causal, ae - src_s[c, g], NEG_BIG))
            ms.append((cbs[g] * lmat).astype(BF16))
            xg = jnp.concatenate([xs_s[2 * g, blk, :], xs_s[2 * g + 1, blk, :]], axis=-1)
            xgs.append(xg)
            yps.append(jnp.exp2(ae) * yis[g] + dskip_ref[g] * xg)
            xds.append((xg * (de * jnp.exp2(total - ae))).astype(BF16))
        for g in range(SSM_GROUPS):
            xb = pad_rows(xgs[g].astype(BF16))
            xdiag = jnp.where(same_head, jnp.concatenate([xb] * SSM_HPG, axis=0), jnp.zeros_like(xb[:1, :1]))
            y_s[g, blk, :] = _dot(ms[g], xdiag) + yps[g]
            et = et_s[c, g]
            decay = jnp.concatenate([jnp.broadcast_to(et[r:r + 1, :], (SSM_HEAD_DIM, SSM_STATE))
                                     for r in range(SSM_HPG)], axis=0)
            state_ref[b, g] = sts[g] * decay + _dot_tn(xds[g], bgs[g])
        return carry

    lax.fori_loop(0, n_blocks, block, 0, unroll=2 if n_blocks % 2 == 0 else 1)

    z = _silu(_dot(h, wz_ref[...]))
    parts = []
    for g in range(SSM_GROUPS):
        yg = y_s[g] * z[:, g * SSM_GROUP_DIM:(g + 1) * SSM_GROUP_DIM]
        parts.append(yg * lax.rsqrt(jnp.mean(yg * yg, axis=-1, keepdims=True) + EPS))
    y = (jnp.concatenate(parts, axis=-1) * ng_ref[...]).astype(BF16)
    out_ssm = _dot(y, wso_ref[...])
    merged = p_ref[...].reshape(rows, D_MODEL) + _sigmoid(_dot(h, wgs_ref[...])) * out_ssm
    mix = _dot(merged.astype(BF16), wo_ref[...])
    xo_ref[...] = (x + _rms(mix, gpost_ref[...])).reshape(nseq, tl, D_MODEL)


def _ssd_constants(rows, q):
    i = jnp.arange(rows)
    tri = ((i[:, None] // q == i[None, :] // q) & (i[None, :] <= i[:, None])).astype(BF16)
    tri3 = jnp.concatenate([tri] * 3, axis=1)
    tri3t = jnp.concatenate([tri.T] * 3, axis=0)
    head_of_lane = jnp.arange(D_INNER) // SSM_HEAD_DIM
    e = (jnp.arange(SSM_HEADS)[:, None] == head_of_lane[None, :]).astype(BF16)
    e3 = jnp.concatenate([e] * 3, axis=0)
    return tri3, tri3t, e3


def _ssd_merge_branch(x, p, hist, state, w, nseq, tl, q):
    bsz, seqlen, _ = x.shape
    rows = nseq * tl
    assert bsz % nseq == 0 and seqlen % tl == 0 and tl % q == 0 and q <= SSM_HEAD_DIM
    assert tl % (SUBLANES * CONV_STEP) == 0
    tile = pl.BlockSpec((nseq, tl, D_MODEL), lambda b, t: (b, t, 0))
    per_seq = lambda r, c: pl.BlockSpec((nseq, r, c), lambda b, t: (b, 0, 0))
    state_spec = pl.BlockSpec((nseq, SSM_GROUPS, SSM_GROUP_DIM, SSM_STATE), lambda b, t: (b, 0, 0, 0))
    tri3, tri3t, e3 = _ssd_constants(rows, q)
    weights = [w['g_mix_pre'], w['w_z'], w['w_xbc'], w['w_dt'], w['w_dt_t'], w['w_gate_ssm'],
               w['ssm_taps'], w['dt_bias_row'], w['dt_bias_col'], w['a_log_row'],
               w['a_log_col'], w['d_skip_wide'], w['ssm_norm_g'], w['w_ssm_out'], w['w_o'], w['g_mix_post'],
               tri3, tri3t, e3]
    return pl.pallas_call(
        functools.partial(_ssd_merge_kernel, q),
        grid=(bsz // nseq, seqlen // tl),
        in_specs=[tile, tile, per_seq(SSM_HIST, SSM_CONV_DIM), state_spec]
        + [_resident(a.shape) for a in weights],
        out_specs=[tile, per_seq(SSM_HIST, SSM_CONV_DIM), state_spec],
        out_shape=[jax.ShapeDtypeStruct(x.shape, F32),
                   jax.ShapeDtypeStruct((bsz, SSM_HIST, SSM_CONV_DIM), F32),
                   jax.ShapeDtypeStruct((bsz, SSM_GROUPS, SSM_GROUP_DIM, SSM_STATE), F32)],
        scratch_shapes=[pltpu.VMEM((nseq, SSM_CONV_DIM // LANES, SSM_PAD + tl, LANES), F32),
                        pltpu.VMEM((D_INNER // LANES, rows, LANES), F32),
                        pltpu.VMEM((SSM_GROUPS, rows, SSM_STATE), F32),
                        pltpu.VMEM((SSM_GROUPS, rows, SSM_STATE), F32),
                        pltpu.VMEM((SSM_GROUPS, rows, SSM_GROUP_DIM), F32),
                        pltpu.VMEM((rows // q, SSM_GROUPS, SSM_HPG, SSM_STATE), F32),
                        pltpu.VMEM((2, rows, 3 * SSM_HEADS), BF16),
                        pltpu.VMEM((rows // q, SSM_GROUPS, 1, SSM_GROUP_DIM), F32)],
        compiler_params=_params(("arbitrary", "arbitrary")),
        name="ssd_merge_branch",
    )(x, p, hist, state, *weights)


def _ffn_kernel(x_ref, gpre_ref, wg_ref, wu_ref, wd_ref, gpost_ref, o_ref):
    x = x_ref[...]
    hn = _rms(x, gpre_ref[...]).astype(BF16)
    a = (_silu(_dot(hn, wg_ref[...])) * _dot(hn, wu_ref[...])).astype(BF16)
    o_ref[...] = x + _rms(_dot(a, wd_ref[...]), gpost_ref[...])


def _ffn(x2d, w, tm):
    rows = x2d.shape[0]
    assert rows % tm == 0
    tile = pl.BlockSpec((tm, D_MODEL), lambda i: (i, 0))
    weights = [w['g_ffn_pre'], w['w_gate'], w['w_up'], w['w_down'], w['g_ffn_post']]
    return pl.pallas_call(
        _ffn_kernel,
        grid=(rows // tm,),
        in_specs=[tile] + [_resident(a.shape) for a in weights],
        out_specs=tile,
        out_shape=jax.ShapeDtypeStruct(x2d.shape, F32),
        compiler_params=_params(("arbitrary",)),
        name="ffn",
    )(x2d, *weights)


def _tile_rows(bsz, seqlen, long_rows, short_rows):
    if seqlen % long_rows == 0:
        return long_rows, 1
    nseq = max(short_rows // seqlen, 1)
    while bsz % nseq:
        nseq -= 1
    return seqlen, nseq


def _layer(x, k_b, v_b, conv_hist, ssm_hist, ssm_state, w, q):
    bsz, seqlen, _ = x.shape
    tl, nseq = _tile_rows(bsz, seqlen, CONV_TILE_ROWS, CONV_SHORT_ROWS)
    p, new_conv = _conv_mem_branch(x, conv_hist, k_b, v_b, w, nseq, tl)
    tl, nseq = _tile_rows(bsz, seqlen, SSD_TILE_ROWS, SSD_SHORT_ROWS)
    state = ssm_state.reshape(bsz, SSM_GROUPS, SSM_GROUP_DIM, SSM_STATE)
    x1, new_ssm_conv, new_state = _ssd_merge_branch(x, p, ssm_hist, state, w, nseq, tl, q)
    x2 = _ffn(x1.reshape(bsz * seqlen, D_MODEL), w, min(512, bsz * seqlen)).reshape(x.shape)
    return x2, new_conv, new_ssm_conv, new_state.reshape(bsz, SSM_HEADS, SSM_HEAD_DIM, SSM_STATE)


def _layer_weights(i, g_mix_pre, w_in, dw_w, dw_b, cln_g, cln_b, w_conv_out, ssm_conv_w, ssm_conv_b,
                   dt_bias, a_log, d_skip, ssm_norm_g, w_ssm_out, w_mem_out, w_o, g_mix_post, g_ffn_pre,
                   w_gate, w_up, w_down, g_ffn_post):
    row = lambda a: a[i].reshape(1, -1).astype(F32)
    col = lambda a: a[i].reshape(-1, 1).astype(F32)
    win = lambda a, b: w_in[i, :, a:b].astype(BF16)
    return {
        'g_mix_pre': row(g_mix_pre),
        'w_glu': win(0, N_GLU),
        'w_z': win(OFF_Z, OFF_XBC),
        'w_xbc': win(OFF_XBC, OFF_DT),
        'w_dt': win(OFF_DT, OFF_Q),
        'w_dt_t': win(OFF_DT, OFF_Q).T,
        'w_q': win(OFF_Q, OFF_GATE),
        'w_gate_conv': win(OFF_GATE, OFF_GATE + D_MODEL),
        'w_gate_ssm': win(OFF_GATE + D_MODEL, OFF_GATE + 2 * D_MODEL),
        'w_gate_mem': win(OFF_GATE + 2 * D_MODEL, OFF_GATE + 3 * D_MODEL),
        'dw_taps': _tap_table(dw_w[i], dw_b[i]), 'cln_g': row(cln_g), 'cln_b': row(cln_b),
        'w_conv_out': w_conv_out[i].astype(BF16),
        'ssm_taps': _tap_table(ssm_conv_w[i], ssm_conv_b[i]),
        'dt_bias_row': row(dt_bias), 'dt_bias_col': col(dt_bias),
        'a_log_row': row(a_log), 'a_log_col': col(a_log),
        'd_skip_wide': jnp.repeat(d_skip[i].astype(F32), SSM_HEAD_DIM).reshape(SSM_GROUPS, 1, SSM_GROUP_DIM),
        'ssm_norm_g': row(ssm_norm_g),
        'w_ssm_out': w_ssm_out[i].astype(BF16),
        'w_mem_out': w_mem_out[i].astype(BF16),
        'w_o': w_o[i].astype(BF16),
        'g_mix_post': row(g_mix_post), 'g_ffn_pre': row(g_ffn_pre),
        'w_gate': w_gate[i].astype(BF16), 'w_up': w_up[i].astype(BF16), 'w_down': w_down[i].astype(BF16),
        'g_ffn_post': row(g_ffn_post),
    }


def kernel(x_prompt, x_sample, mem_prompt, cache_mem_k, cache_mem_v, state_conv, state_ssm_conv, state_ssm, g_mix_pre, w_in, dw_w, dw_b, cln_g, cln_b, w_conv_out, ssm_conv_w, ssm_conv_b, dt_bias, a_log, d_skip, ssm_norm_g, w_ssm_out, mem_norm_g, w_mem_k, w_mem_v, w_mem_out, w_o, g_mix_post, g_ffn_pre, w_gate, w_up, w_down, g_ffn_post):
    depth = w_in.shape[0]
    bp, lp, _ = x_prompt.shape
    bs, ls, _ = x_sample.shape
    q_p = SSD_BLOCK if lp % SSD_BLOCK == 0 else lp
    q_s = SSD_BLOCK if ls % SSD_BLOCK == 0 else ls
    xp, xs = x_prompt, x_sample
    mem2d = mem_prompt.reshape(bp * N_MEM, D_MODEL)
    zc = jnp.zeros((bp, CONV_HIST, D_CONV), F32)
    zsc = jnp.zeros((bp, SSM_HIST, SSM_CONV_DIM), F32)
    zs = jnp.zeros((bp, SSM_HEADS, SSM_HEAD_DIM, SSM_STATE), F32)
    outs = [[] for _ in range(8)]
    for i in range(depth):
        w = _layer_weights(i, g_mix_pre, w_in, dw_w, dw_b, cln_g, cln_b, w_conv_out, ssm_conv_w, ssm_conv_b,
                           dt_bias, a_log, d_skip, ssm_norm_g, w_ssm_out, w_mem_out, w_o, g_mix_post,
                           g_ffn_pre, w_gate, w_up, w_down, g_ffn_post)
        w_kv = jnp.concatenate([w_mem_k[i], w_mem_v[i]], axis=1).astype(BF16)
        mk, mv, mk_b, mv_b = _memory_kv(mem2d, mem_norm_g[i].reshape(1, D_MODEL).astype(F32), w_kv)
        xp, c_new, sc_new, s_new = _layer(xp, mk_b.reshape(bp, N_MEM, D_MEM), mv_b.reshape(bp, N_MEM, D_MEM),
                                          zc, zsc, zs, w, q_p)
        outs[0].append(mk.reshape(bp, N_MEM, MEM_HEADS, MEM_HEAD_DIM))
        outs[1].append(mv.reshape(bp, N_MEM, MEM_HEADS, MEM_HEAD_DIM))
        outs[2].append(c_new)
        outs[3].append(sc_new)
        outs[4].append(s_new)
        ck = cache_mem_k[i].reshape(bs, N_MEM, D_MEM).astype(BF16)
        cv = cache_mem_v[i].reshape(bs, N_MEM, D_MEM).astype(BF16)
        xs, c_new, sc_new, s_new = _layer(xs, ck, cv, state_conv[i], state_ssm_conv[i], state_ssm[i], w, q_s)
        outs[5].append(c_new)
        outs[6].append(sc_new)
        outs[7].append(s_new)
    return (xp, xs) + tuple(jnp.stack(o) for o in outs)
```

```python
import functools

import jax
import jax.numpy as jnp
from jax import lax
from jax.experimental import pallas as pl
from jax.experimental.pallas import tpu as pltpu

F32 = jnp.float32
BF16 = jnp.bfloat16

EPS = 1e-6
D_MODEL = 1024
D_CONV = D_MODEL
CONV_WIDTH = 31
CONV_HIST = CONV_WIDTH - 1
D_INNER = 2 * D_MODEL
SSM_HEAD_DIM = 64
SSM_HEADS = D_INNER // SSM_HEAD_DIM
SSM_GROUPS = 8
SSM_HPG = SSM_HEADS // SSM_GROUPS
SSM_GROUP_DIM = SSM_HPG * SSM_HEAD_DIM
SSM_STATE = 128
SSM_CONV_WIDTH = 4
SSM_HIST = SSM_CONV_WIDTH - 1
SSM_CONV_DIM = D_INNER + 2 * SSM_GROUPS * SSM_STATE
SSD_BLOCK = 64
N_MEM = 256
MEM_HEADS = 4
MEM_HEAD_DIM = 128
D_MEM = MEM_HEADS * MEM_HEAD_DIM
D_FF = 2816
N_GLU = 2 * D_CONV
OFF_Z = N_GLU
OFF_XBC = OFF_Z + D_INNER
OFF_DT = OFF_XBC + SSM_CONV_DIM
OFF_Q = OFF_DT + SSM_HEADS
OFF_GATE = OFF_Q + D_MEM

SUBLANES = 8
LANES = 128
CONV_PAD = 32
SSM_PAD = 8
ROW_BLOCK = 32
CONV_STEP = 4
STAGE_EVERY = 4
CONV_TILE_ROWS = 1024
SSD_TILE_ROWS = 256
CONV_SHORT_ROWS = 256
SSD_SHORT_ROWS = 128
VMEM_LIMIT_BYTES = 56 * 1024 * 1024
LOG2_E = 1.4426950408889634
NEG_BIG = -1e30


def _dot(a, b):
    return jnp.dot(a, b, preferred_element_type=F32)


def _dot_nt(a, b):
    return lax.dot_general(a, b, (((1,), (1,)), ((), ())), preferred_element_type=F32)


def _dot_tn(a, b):
    return lax.dot_general(a, b, (((0,), (0,)), ((), ())), preferred_element_type=F32)


def _rms(x, g):
    return x * lax.rsqrt(jnp.mean(x * x, axis=-1, keepdims=True) + EPS) * g


def _sigmoid(x):
    return 1.0 / (1.0 + jnp.exp(-x))


def _silu(x):
    half = 0.5 * x
    return half + half * jnp.tanh(half)


def _softplus(x):
    return jnp.maximum(x, 0.0) + jnp.log1p(jnp.exp(-jnp.abs(x)))


def _bf16_pieces(x):
    hi = x.astype(BF16)
    r1 = x - hi.astype(F32)
    mid = r1.astype(BF16)
    lo = (r1 - mid.astype(F32)).astype(BF16)
    return [hi, mid, lo]


def _resident(shape):
    nd = len(shape)
    return pl.BlockSpec(shape, lambda *_: (0,) * nd, pipeline_mode=pl.Buffered(1))


def _resident_columns(arr, start, width):
    assert start % width == 0 and width % LANES == 0
    return pl.BlockSpec((arr.shape[0], width), lambda *_: (0, start // width), pipeline_mode=pl.Buffered(1))


def _params(sem):
    return pltpu.CompilerParams(dimension_semantics=sem, vmem_limit_bytes=VMEM_LIMIT_BYTES)


def _memkv_kernel(mem_ref, g_ref, w_ref, k_ref, v_ref, kb_ref, vb_ref):
    h = _rms(mem_ref[...], g_ref[...]).astype(BF16)
    kv = _dot(h, w_ref[...])
    k = kv[:, :D_MEM]
    v = kv[:, D_MEM:]
    k_ref[...] = k
    v_ref[...] = v
    kb_ref[...] = k.astype(BF16)
    vb_ref[...] = v.astype(BF16)


def _memory_kv(mem2d, g, w_kv):
    rows = mem2d.shape[0]
    tm = 512
    assert rows % tm == 0
    row_spec = pl.BlockSpec((tm, D_MODEL), lambda i: (i, 0))
    out_spec = pl.BlockSpec((tm, D_MEM), lambda i: (i, 0))
    return pl.pallas_call(
        _memkv_kernel,
        grid=(rows // tm,),
        in_specs=[row_spec, _resident((1, D_MODEL)), _resident((D_MODEL, 2 * D_MEM))],
        out_specs=[out_spec] * 4,
        out_shape=[jax.ShapeDtypeStruct((rows, D_MEM), F32)] * 2
        + [jax.ShapeDtypeStruct((rows, D_MEM), BF16)] * 2,
        compiler_params=_params(("arbitrary",)),
        name="memory_kv",
    )(mem2d, g, w_kv)


def _conv_span(tl):
    blk = SUBLANES * CONV_STEP
    return 2 * blk if tl % (2 * blk) == 0 else blk


def _strided_depthwise(load, put, w_ref, lane_tile, width, r0, span, finish):
    blk = SUBLANES * CONV_STEP
    starts = [s0 + j for s0 in range(r0, r0 + span, blk) for j in range(CONV_STEP)]
    accs = [w_ref[lane_tile, width]] * len(starts)
    for k in range(width):
        wk = w_ref[lane_tile, k]
        accs = [a + wk * load(s0 + k, CONV_STEP) for a, s0 in zip(accs, starts)]
    for a, s0 in zip(accs, starts):
        put(s0, CONV_STEP, finish(a))


def _strided_rows(start, stride):
    return pl.ds(start, SUBLANES, stride=stride)


def _tap_table(w, b):
    wb = jnp.concatenate([w, b[None, :]], axis=0).astype(F32)
    wb = wb.reshape(wb.shape[0], -1, 1, LANES).transpose(1, 0, 2, 3)
    return jnp.broadcast_to(wb, wb.shape[:2] + (SUBLANES, LANES))


def _conv_mem_kernel(x_ref, hist_ref, k_ref, v_ref, g_ref, wglu_ref, wgc_ref, wq_ref, wgm_ref,
                     dww_ref, clng_ref, clnb_ref, wco_ref, wmo_ref,
                     p_ref, hist_out_ref, ubuf, ybuf, act):
    nseq, tl = x_ref.shape[0], x_ref.shape[1]
    rows = nseq * tl
    t = pl.program_id(1)
    n_tiles = D_CONV // LANES
    h = _rms(x_ref[...].reshape(rows, D_MODEL), g_ref[...]).astype(BF16)

    glu = _dot(h, wglu_ref[...])
    u = glu[:, :D_CONV] * _sigmoid(glu[:, D_CONV:])

    @pl.when(t == 0)
    def _():
        for b in range(nseq):
            for ct in range(n_tiles):
                ubuf[b, ct, CONV_PAD - CONV_HIST:CONV_PAD, :] = hist_ref[b, :, ct * LANES:(ct + 1) * LANES]

    span = _conv_span(tl)
    for b in range(nseq):
        for ct in range(n_tiles):
            ubuf[b, ct, CONV_PAD:CONV_PAD + tl, :] = u[b * tl:(b + 1) * tl, ct * LANES:(ct + 1) * LANES]
        for r0 in range(0, tl, span):
            for ct in range(n_tiles):
                def load(start, stride, b=b, ct=ct):
                    return ubuf[b, ct, _strided_rows(CONV_PAD - CONV_HIST + start, stride), :]

                def put(start, stride, val, b=b, ct=ct):
                    ybuf[ct, _strided_rows(b * tl + start, stride), :] = val

                _strided_depthwise(load, put, dww_ref, ct, CONV_WIDTH, r0, span, lambda a: a)
            for r1 in range(b * tl + r0, b * tl + r0 + span, ROW_BLOCK):
                y = jnp.concatenate([ybuf[ct, r1:r1 + ROW_BLOCK, :] for ct in range(n_tiles)], axis=-1)
                mu = jnp.mean(y, axis=-1, keepdims=True)
                yc = y - mu
                var = jnp.mean(yc * yc, axis=-1, keepdims=True)
                yn = yc * lax.rsqrt(var + EPS) * clng_ref[...] + clnb_ref[...]
                act[r1:r1 + ROW_BLOCK, :] = _silu(yn).astype(BF16)
        for ct in range(n_tiles):
            new_hist = ubuf[b, ct, tl + CONV_PAD - CONV_HIST:tl + CONV_PAD, :]
            ubuf[b, ct, CONV_PAD - CONV_HIST:CONV_PAD, :] = new_hist
            hist_out_ref[b, :, ct * LANES:(ct + 1) * LANES] = new_hist

    out_conv = _dot(act[...], wco_ref[...])
    merged = _sigmoid(_dot(h, wgc_ref[...])) * out_conv

    q = _dot(h, wq_ref[...]).astype(BF16)
    per_seq = []
    for b in range(nseq):
        heads = []
        for hd in range(MEM_HEADS):
            sl = slice(hd * MEM_HEAD_DIM, (hd + 1) * MEM_HEAD_DIM)
            s = _dot_nt(q[b * tl:(b + 1) * tl, sl], k_ref[b, :, sl]) * (MEM_HEAD_DIM ** -0.5)
            e = jnp.exp(s - jnp.max(s, axis=-1, keepdims=True))
            heads.append(_dot(e.astype(BF16), v_ref[b, :, sl]) / jnp.sum(e, axis=-1, keepdims=True))
        per_seq.append(jnp.concatenate(heads, axis=-1))
    o = jnp.concatenate(per_seq, axis=0).astype(BF16)
    out_mem = _dot(o, wmo_ref[...])
    merged = merged + _sigmoid(_dot(h, wgm_ref[...])) * out_mem
    p_ref[...] = merged.reshape(nseq, tl, D_MODEL)


def _conv_mem_branch(x, hist, k_b, v_b, w, nseq, tl):
    bsz, seqlen, _ = x.shape
    rows = nseq * tl
    assert bsz % nseq == 0 and seqlen % tl == 0 and tl >= CONV_HIST
    assert tl % ROW_BLOCK == 0 and tl % (SUBLANES * CONV_STEP) == 0
    tile = pl.BlockSpec((nseq, tl, D_MODEL), lambda b, t: (b, t, 0))
    per_seq = lambda r, c: pl.BlockSpec((nseq, r, c), lambda b, t: (b, 0, 0))
    weights = [w['g_mix_pre'], w['w_in'], w['w_gate_conv'], w['w_q'], w['w_gate_mem'],
               w['dw_taps'], w['cln_g'], w['cln_b'], w['w_conv_out'], w['w_mem_out']]
    weight_specs = [_resident(a.shape) for a in weights]
    weight_specs[1] = _resident_columns(w['w_in'], 0, N_GLU)
    return pl.pallas_call(
        _conv_mem_kernel,
        grid=(bsz // nseq, seqlen // tl),
        in_specs=[tile, per_seq(CONV_HIST, D_CONV), per_seq(N_MEM, D_MEM), per_seq(N_MEM, D_MEM)] + weight_specs,
        out_specs=[tile, per_seq(CONV_HIST, D_CONV)],
        out_shape=[jax.ShapeDtypeStruct(x.shape, F32),
                   jax.ShapeDtypeStruct((bsz, CONV_HIST, D_CONV), F32)],
        scratch_shapes=[pltpu.VMEM((nseq, D_CONV // LANES, CONV_PAD + tl, LANES), F32),
                        pltpu.VMEM((D_CONV // LANES, rows, LANES), F32),
                        pltpu.VMEM((rows, D_CONV), BF16)],
        compiler_params=_params(("arbitrary", "arbitrary")),
        name="conv_mem_branch",
    )(x, hist, k_b, v_b, *weights)


def _ssd_merge_kernel(q, x_ref, p_ref, hist_ref, state_in_ref, g_ref, wz_ref, wxbc_ref, wdt_ref, wdtt_ref,
                      wgs_ref, cw_ref, dtb_row_ref, dtb_col_ref, alog_row_ref, alog_col_ref,
                      dskip_ref, ng_ref, wso_ref, wo_ref, gpost_ref, tri3_ref, tri3t_ref, e3_ref,
                      xo_ref, hist_out_ref, state_ref, cbuf, xs_s, b_s, c_s, y_s, et_s, spread_s, src_s):
    nseq, tl = x_ref.shape[0], x_ref.shape[1]
    rows = nseq * tl
    t = pl.program_id(1)
    n_tiles = SSM_CONV_DIM // LANES
    n_x_tiles = D_INNER // LANES
    n_b_tiles = SSM_GROUPS * SSM_STATE // LANES
    n_blocks = rows // q
    blocks_per_seq = tl // q
    slot = SSM_HEAD_DIM
    x = x_ref[...].reshape(rows, D_MODEL)
    h = _rms(x, g_ref[...]).astype(BF16)

    v = {}

    def stage_proj():
        v['dt_raw'] = _dot(h, wdt_ref[...])
        v['dtt_raw'] = _dot_nt(wdtt_ref[...], h)

    def stage_softplus():
        a_row = -jnp.exp(alog_row_ref[...]) * LOG2_E
        a_col = -jnp.exp(alog_col_ref[...]) * LOG2_E
        v['dt'] = _softplus(v['dt_raw'] + dtb_row_ref[...])
        v['dt_t'] = _softplus(v['dtt_raw'] + dtb_col_ref[...])
        v['da'] = v['dt'] * a_row
        v['da_t'] = v['dt_t'] * a_col

    def stage_cumsum():
        v['acum'] = _dot(tri3_ref[...], jnp.concatenate(_bf16_pieces(v['da']), axis=0))
        v['acum_t'] = _dot(jnp.concatenate(_bf16_pieces(v['da_t']), axis=-1), tri3t_ref[...])

    def stage_spread():
        spread_s[0] = jnp.concatenate(_bf16_pieces(v['dt']), axis=-1)
        spread_s[1] = jnp.concatenate(_bf16_pieces(v['acum']), axis=-1)
        v['src_t'] = v['acum_t'] - jnp.log2(v['dt_t'])
        for c in range(n_blocks):
            tot = jnp.exp2(jnp.sum(v['da_t'][:, c * q:(c + 1) * q], axis=1, keepdims=True))
            tot = jnp.broadcast_to(tot, (SSM_HEADS, SSM_STATE))
            for g in range(SSM_GROUPS):
                et_s[c, g] = tot[g * SSM_HPG:(g + 1) * SSM_HPG, :]

    def stage_rows(c):
        def run():
            for g in range(SSM_GROUPS):
                pieces = []
                for r in range(SSM_HPG):
                    hh = g * SSM_HPG + r
                    pieces.append(v['src_t'][hh:hh + 1, c * q:(c + 1) * q])
                    if q < slot:
                        pieces.append(jnp.zeros((1, slot - q), F32))
                src_s[c, g] = jnp.concatenate(pieces, axis=-1)
        return run

    stages = [stage_proj, stage_softplus, stage_cumsum, stage_spread] + [stage_rows(c) for c in range(n_blocks)]

    @pl.when(t == 0)
    def _():
        for b in range(nseq):
            for ct in range(n_tiles):
                cbuf[b, ct, SSM_PAD - SSM_HIST:SSM_PAD, :] = hist_ref[b, :, ct * LANES:(ct + 1) * LANES]
        state_ref[...] = state_in_ref[...]

    def xbc_dest(ct):
        if ct < n_x_tiles:
            return xs_s, ct
        if ct < n_x_tiles + n_b_tiles:
            return b_s, ct - n_x_tiles
        return c_s, ct - n_x_tiles - n_b_tiles

    xbc = _dot(h, wxbc_ref[...])
    span = _conv_span(tl)
    for ct in range(n_tiles):
        cs = slice(ct * LANES, (ct + 1) * LANES)
        dst, di = xbc_dest(ct)
        for b in range(nseq):
            cbuf[b, ct, SSM_PAD:SSM_PAD + tl, :] = xbc[b * tl:(b + 1) * tl, cs]

            def load(start, stride, b=b):
                return cbuf[b, ct, _strided_rows(SSM_PAD - SSM_HIST + start, stride), :]

            def put(start, stride, val, b=b):
                dst[di, _strided_rows(b * tl + start, stride), :] = val

            for r0 in range(0, tl, span):
                _strided_depthwise(load, put, cw_ref, ct, SSM_CONV_WIDTH, r0, span, _silu)
            new_hist = cbuf[b, ct, tl + SSM_PAD - SSM_HIST:tl + SSM_PAD, :]
            cbuf[b, ct, SSM_PAD - SSM_HIST:SSM_PAD, :] = new_hist
            hist_out_ref[b, :, cs] = new_hist
        if ct % STAGE_EVERY == STAGE_EVERY - 1 and ct // STAGE_EVERY < len(stages):
            stages[ct // STAGE_EVERY]()
    for stage in stages[n_tiles // STAGE_EVERY:]:
        stage()

    li = lax.broadcasted_iota(jnp.int32, (q, SSM_GROUP_DIM), 0)
    si = lax.broadcasted_iota(jnp.int32, (q, SSM_GROUP_DIM), 1) % slot
    causal = li >= si
    rr = lax.broadcasted_iota(jnp.int32, (SSM_GROUP_DIM, SSM_GROUP_DIM), 0) // slot
    rc = lax.broadcasted_iota(jnp.int32, (SSM_GROUP_DIM, SSM_GROUP_DIM), 1) // SSM_HEAD_DIM
    same_head = rr == rc

    def pad_rows(a):
        if q == slot:
            return a
        return jnp.concatenate([a, jnp.zeros((slot - q, a.shape[1]), a.dtype)], axis=0)

    def block(c, carry):
        b = c // blocks_per_seq
        blk = pl.ds(pl.multiple_of(c * q, q), q)
        ex = _dot(jnp.concatenate([spread_s[0, blk, :], spread_s[1, blk, :]], axis=0), e3_ref[...])
        cgs = [c_s[g, blk, :].astype(BF16) for g in range(SSM_GROUPS)]
        bgs = [b_s[g, blk, :].astype(BF16) for g in range(SSM_GROUPS)]
        sts = [state_ref[b, g] for g in range(SSM_GROUPS)]
        cbs = [_dot_nt(cg, jnp.concatenate([pad_rows(bg)] * SSM_HPG, axis=0)) for cg, bg in zip(cgs, bgs)]
        yis = [_dot_nt(cg, st.astype(BF16)) for cg, st in zip(cgs, sts)]
        ms, yps, xds, xgs = [], [], [], []
        for g in range(SSM_GROUPS):
            gs = slice(g * SSM_GROUP_DIM, (g + 1) * SSM_GROUP_DIM)
            de = ex[:q, gs]
            ae = ex[q:, gs]
            total = ae[q - 1:q, :]
            lmat = jnp.exp2(jnp.where(causal, ae - src_s[c, g], NEG_BIG))
            ms.append((cbs[g] * lmat).astype(BF16))
            xg = jnp.concatenate([xs_s[2 * g, blk, :], xs_s[2 * g + 1, blk, :]], axis=-1)
            xgs.append(xg)
            yps.append(jnp.exp2(ae) * yis[g] + dskip_ref[g] * xg)
            xds.append((xg * (de * jnp.exp2(total - ae))).astype(BF16))
        for g in range(SSM_GROUPS):
            xb = pad_rows(xgs[g].astype(BF16))
            xdiag = jnp.where(same_head, jnp.concatenate([xb] * SSM_HPG, axis=0), jnp.zeros_like(xb[:1, :1]))
            y_s[g, blk, :] = _dot(ms[g], xdiag) + yps[g]
            et = et_s[c, g]
            decay = jnp.concatenate([jnp.broadcast_to(et[r:r + 1, :], (SSM_HEAD_DIM, SSM_STATE))
                                     for r in range(SSM_HPG)], axis=0)
            state_ref[b, g] = sts[g] * decay + _dot_tn(xds[g], bgs[g])
        return carry

    lax.fori_loop(0, n_blocks, block, 0, unroll=2 if n_blocks % 2 == 0 else 1)

    z = _silu(_dot(h, wz_ref[...]))
    parts = []
    for g in range(SSM_GROUPS):
        yg = y_s[g] * z[:, g * SSM_GROUP_DIM:(g + 1) * SSM_GROUP_DIM]
        parts.append(yg * lax.rsqrt(jnp.mean(yg * yg, axis=-1, keepdims=True) + EPS))
    y = (jnp.concatenate(parts, axis=-1) * ng_ref[...]).astype(BF16)
    out_ssm = _dot(y, wso_ref[...])
    merged = p_ref[...].reshape(rows, D_MODEL) + _sigmoid(_dot(h, wgs_ref[...])) * out_ssm
    mix = _dot(merged.astype(BF16), wo_ref[...])
    xo_ref[...] = (x + _rms(mix, gpost_ref[...])).reshape(nseq, tl, D_MODEL)


def _ssd_constants(rows, q):
    i = jnp.arange(rows)
    tri = ((i[:, None] // q == i[None, :] // q) & (i[None, :] <= i[:, None])).astype(BF16)
    tri3 = jnp.concatenate([tri] * 3, axis=1)
    tri3t = jnp.concatenate([tri.T] * 3, axis=0)
    head_of_lane = jnp.arange(D_INNER) // SSM_HEAD_DIM
    e = (jnp.arange(SSM_HEADS)[:, None] == head_of_lane[None, :]).astype(BF16)
    e3 = jnp.concatenate([e] * 3, axis=0)
    return tri3, tri3t, e3


def _ssd_merge_branch(x, p, hist, state, w, nseq, tl, q):
    bsz, seqlen, _ = x.shape
    rows = nseq * tl
    assert bsz % nseq == 0 and seqlen % tl == 0 and tl % q == 0 and q <= SSM_HEAD_DIM
    assert tl % (SUBLANES * CONV_STEP) == 0
    tile = pl.BlockSpec((nseq, tl, D_MODEL), lambda b, t: (b, t, 0))
    per_seq = lambda r, c: pl.BlockSpec((nseq, r, c), lambda b, t: (b, 0, 0))
    state_spec = pl.BlockSpec((nseq, SSM_GROUPS, SSM_GROUP_DIM, SSM_STATE), lambda b, t: (b, 0, 0, 0))
    tri3, tri3t, e3 = _ssd_constants(rows, q)
    weights = [w['g_mix_pre'], w['w_in'], w['w_in'], w['w_dt'], w['w_dt_t'], w['w_gate_ssm'],
               w['ssm_taps'], w['dt_bias_row'], w['dt_bias_col'], w['a_log_row'],
               w['a_log_col'], w['d_skip_wide'], w['ssm_norm_g'], w['w_ssm_out'], w['w_o'], w['g_mix_post'],
               tri3, tri3t, e3]
    weight_specs = [_resident(a.shape) for a in weights]
    weight_specs[1] = _resident_columns(w['w_in'], OFF_Z, D_INNER)
    weight_specs[2] = _resident_columns(w['w_in'], OFF_XBC, SSM_CONV_DIM)
    return pl.pallas_call(
        functools.partial(_ssd_merge_kernel, q),
        grid=(bsz // nseq, seqlen // tl),
        in_specs=[tile, tile, per_seq(SSM_HIST, SSM_CONV_DIM), state_spec] + weight_specs,
        out_specs=[tile, per_seq(SSM_HIST, SSM_CONV_DIM), state_spec],
        out_shape=[jax.ShapeDtypeStruct(x.shape, F32),
                   jax.ShapeDtypeStruct((bsz, SSM_HIST, SSM_CONV_DIM), F32),
                   jax.ShapeDtypeStruct((bsz, SSM_GROUPS, SSM_GROUP_DIM, SSM_STATE), F32)],
        scratch_shapes=[pltpu.VMEM((nseq, SSM_CONV_DIM // LANES, SSM_PAD + tl, LANES), F32),
                        pltpu.VMEM((D_INNER // LANES, rows, LANES), F32),
                        pltpu.VMEM((SSM_GROUPS, rows, SSM_STATE), F32),
                        pltpu.VMEM((SSM_GROUPS, rows, SSM_STATE), F32),
                        pltpu.VMEM((SSM_GROUPS, rows, SSM_GROUP_DIM), F32),
                        pltpu.VMEM((rows // q, SSM_GROUPS, SSM_HPG, SSM_STATE), F32),
                        pltpu.VMEM((2, rows, 3 * SSM_HEADS), BF16),
                        pltpu.VMEM((rows // q, SSM_GROUPS, 1, SSM_GROUP_DIM), F32)],
        compiler_params=_params(("arbitrary", "arbitrary")),
        name="ssd_merge_branch",
    )(x, p, hist, state, *weights)


def _ffn_kernel(x_ref, gpre_ref, wg_ref, wu_ref, wd_ref, gpost_ref, o_ref):
    x = x_ref[...]
    hn = _rms(x, gpre_ref[...]).astype(BF16)
    a = (_silu(_dot(hn, wg_ref[...])) * _dot(hn, wu_ref[...])).astype(BF16)
    o_ref[...] = x + _rms(_dot(a, wd_ref[...]), gpost_ref[...])


def _ffn(x2d, w, tm):
    rows = x2d.shape[0]
    assert rows % tm == 0
    tile = pl.BlockSpec((tm, D_MODEL), lambda i: (i, 0))
    weights = [w['g_ffn_pre'], w['w_gate'], w['w_up'], w['w_down'], w['g_ffn_post']]
    return pl.pallas_call(
        _ffn_kernel,
        grid=(rows // tm,),
        in_specs=[tile] + [_resident(a.shape) for a in weights],
        out_specs=tile,
        out_shape=jax.ShapeDtypeStruct(x2d.shape, F32),
        compiler_params=_params(("arbitrary",)),
        name="ffn",
    )(x2d, *weights)


def _tile_rows(bsz, seqlen, long_rows, short_rows):
    if seqlen % long_rows == 0:
        return long_rows, 1
    nseq = max(short_rows // seqlen, 1)
    while bsz % nseq:
        nseq -= 1
    return seqlen, nseq


def _layer(x, k_b, v_b, conv_hist, ssm_hist, ssm_state, w, q):
    bsz, seqlen, _ = x.shape
    tl, nseq = _tile_rows(bsz, seqlen, CONV_TILE_ROWS, CONV_SHORT_ROWS)
    p, new_conv = _conv_mem_branch(x, conv_hist, k_b, v_b, w, nseq, tl)
    tl, nseq = _tile_rows(bsz, seqlen, SSD_TILE_ROWS, SSD_SHORT_ROWS)
    state = ssm_state.reshape(bsz, SSM_GROUPS, SSM_GROUP_DIM, SSM_STATE)
    x1, new_ssm_conv, new_state = _ssd_merge_branch(x, p, ssm_hist, state, w, nseq, tl, q)
    x2 = _ffn(x1.reshape(bsz * seqlen, D_MODEL), w, min(512, bsz * seqlen)).reshape(x.shape)
    return x2, new_conv, new_ssm_conv, new_state.reshape(bsz, SSM_HEADS, SSM_HEAD_DIM, SSM_STATE)


def _layer_weights(i, g_mix_pre, w_in, dw_w, dw_b, cln_g, cln_b, w_conv_out, ssm_conv_w, ssm_conv_b,
                   dt_bias, a_log, d_skip, ssm_norm_g, w_ssm_out, w_mem_out, w_o, g_mix_post, g_ffn_pre,
                   w_gate, w_up, w_down, g_ffn_post):
    row = lambda a: a[i].reshape(1, -1).astype(F32)
    col = lambda a: a[i].reshape(-1, 1).astype(F32)
    win = lambda a, b: w_in[i, :, a:b].astype(BF16)
    return {
        'g_mix_pre': row(g_mix_pre),
        'w_in': w_in[i].astype(BF16),
        'w_dt': win(OFF_DT, OFF_Q),
        'w_dt_t': win(OFF_DT, OFF_Q).T,
        'w_q': win(OFF_Q, OFF_GATE),
        'w_gate_conv': win(OFF_GATE, OFF_GATE + D_MODEL),
        'w_gate_ssm': win(OFF_GATE + D_MODEL, OFF_GATE + 2 * D_MODEL),
        'w_gate_mem': win(OFF_GATE + 2 * D_MODEL, OFF_GATE + 3 * D_MODEL),
        'dw_taps': _tap_table(dw_w[i], dw_b[i]), 'cln_g': row(cln_g), 'cln_b': row(cln_b),
        'w_conv_out': w_conv_out[i].astype(BF16),
        'ssm_taps': _tap_table(ssm_conv_w[i], ssm_conv_b[i]),
        'dt_bias_row': row(dt_bias), 'dt_bias_col': col(dt_bias),
        'a_log_row': row(a_log), 'a_log_col': col(a_log),
        'd_skip_wide': jnp.repeat(d_skip[i].astype(F32), SSM_HEAD_DIM).reshape(SSM_GROUPS, 1, SSM_GROUP_DIM),
        'ssm_norm_g': row(ssm_norm_g),
        'w_ssm_out': w_ssm_out[i].astype(BF16),
        'w_mem_out': w_mem_out[i].astype(BF16),
        'w_o': w_o[i].astype(BF16),
        'g_mix_post': row(g_mix_post), 'g_ffn_pre': row(g_ffn_pre),
        'w_gate': w_gate[i].astype(BF16), 'w_up': w_up[i].astype(BF16), 'w_down': w_down[i].astype(BF16),
        'g_ffn_post': row(g_ffn_post),
    }


def kernel(x_prompt, x_sample, mem_prompt, cache_mem_k, cache_mem_v, state_conv, state_ssm_conv, state_ssm, g_mix_pre, w_in, dw_w, dw_b, cln_g, cln_b, w_conv_out, ssm_conv_w, ssm_conv_b, dt_bias, a_log, d_skip, ssm_norm_g, w_ssm_out, mem_norm_g, w_mem_k, w_mem_v, w_mem_out, w_o, g_mix_post, g_ffn_pre, w_gate, w_up, w_down, g_ffn_post):
    depth = w_in.shape[0]
    bp, lp, _ = x_prompt.shape
    bs, ls, _ = x_sample.shape
    q_p = SSD_BLOCK if lp % SSD_BLOCK == 0 else lp
    q_s = SSD_BLOCK if ls % SSD_BLOCK == 0 else ls
    xp, xs = x_prompt, x_sample
    mem2d = mem_prompt.reshape(bp * N_MEM, D_MODEL)
    zc = jnp.zeros((bp, CONV_HIST, D_CONV), F32)
    zsc = jnp.zeros((bp, SSM_HIST, SSM_CONV_DIM), F32)
    zs = jnp.zeros((bp, SSM_HEADS, SSM_HEAD_DIM, SSM_STATE), F32)
    outs = [[] for _ in range(8)]
    for i in range(depth):
        w = _layer_weights(i, g_mix_pre, w_in, dw_w, dw_b, cln_g, cln_b, w_conv_out, ssm_conv_w, ssm_conv_b,
                           dt_bias, a_log, d_skip, ssm_norm_g, w_ssm_out, w_mem_out, w_o, g_mix_post,
                           g_ffn_pre, w_gate, w_up, w_down, g_ffn_post)
        w_kv = jnp.concatenate([w_mem_k[i], w_mem_v[i]], axis=1).astype(BF16)
        mk, mv, mk_b, mv_b = _memory_kv(mem2d, mem_norm_g[i].reshape(1, D_MODEL).astype(F32), w_kv)
        xp, c_new, sc_new, s_new = _layer(xp, mk_b.reshape(bp, N_MEM, D_MEM), mv_b.reshape(bp, N_MEM, D_MEM),
                                          zc, zsc, zs, w, q_p)
        outs[0].append(mk.reshape(bp, N_MEM, MEM_HEADS, MEM_HEAD_DIM))
        outs[1].append(mv.reshape(bp, N_MEM, MEM_HEADS, MEM_HEAD_DIM))
        outs[2].append(c_new)
        outs[3].append(sc_new)
        outs[4].append(s_new)
        ck = cache_mem_k[i].reshape(bs, N_MEM, D_MEM).astype(BF16)
        cv = cache_mem_v[i].reshape(bs, N_MEM, D_MEM).astype(BF16)
        xs, c_new, sc_new, s_new = _layer(xs, ck, cv, state_conv[i], state_ssm_conv[i], state_ssm[i], w, q_s)
        outs[5].append(c_new)
        outs[6].append(sc_new)
        outs[7].append(s_new)
    return (xp, xs) + tuple(jnp.stack(o) for o in outs)
```

```python
import functools

import jax
import jax.numpy as jnp
from jax import lax
from jax.experimental import pallas as pl
from jax.experimental.pallas import tpu as pltpu

F32 = jnp.float32
BF16 = jnp.bfloat16

EPS = 1e-6
D_MODEL = 1024
D_CONV = D_MODEL
CONV_WIDTH = 31
CONV_HIST = CONV_WIDTH - 1
D_INNER = 2 * D_MODEL
SSM_HEAD_DIM = 64
SSM_HEADS = D_INNER // SSM_HEAD_DIM
SSM_GROUPS = 8
SSM_HPG = SSM_HEADS // SSM_GROUPS
SSM_GROUP_DIM = SSM_HPG * SSM_HEAD_DIM
SSM_STATE = 128
SSM_CONV_WIDTH = 4
SSM_HIST = SSM_CONV_WIDTH - 1
SSM_CONV_DIM = D_INNER + 2 * SSM_GROUPS * SSM_STATE
SSD_BLOCK = 64
N_MEM = 256
MEM_HEADS = 4
MEM_HEAD_DIM = 128
D_MEM = MEM_HEADS * MEM_HEAD_DIM
D_FF = 2816
N_GLU = 2 * D_CONV
OFF_Z = N_GLU
OFF_XBC = OFF_Z + D_INNER
OFF_DT = OFF_XBC + SSM_CONV_DIM
OFF_Q = OFF_DT + SSM_HEADS
OFF_GATE = OFF_Q + D_MEM

SUBLANES = 8
LANES = 128
CONV_PAD = 32
SSM_PAD = 8
ROW_BLOCK = 32
CONV_STEP = 4
STAGE_EVERY = 4
CONV_TILE_ROWS = 1024
SSD_TILE_ROWS = 256
CONV_SHORT_ROWS = 256
SSD_SHORT_ROWS = 128
VMEM_LIMIT_BYTES = 56 * 1024 * 1024
LOG2_E = 1.4426950408889634
NEG_BIG = -1e30


def _dot(a, b):
    return jnp.dot(a, b, preferred_element_type=F32)


def _dot_nt(a, b):
    return lax.dot_general(a, b, (((1,), (1,)), ((), ())), preferred_element_type=F32)


def _dot_tn(a, b):
    return lax.dot_general(a, b, (((0,), (0,)), ((), ())), preferred_element_type=F32)


def _rms(x, g):
    return x * lax.rsqrt(jnp.mean(x * x, axis=-1, keepdims=True) + EPS) * g


def _sigmoid(x):
    return 1.0 / (1.0 + jnp.exp(-x))


def _silu(x):
    half = 0.5 * x
    return half + half * jnp.tanh(half)


def _softplus(x):
    return jnp.maximum(x, 0.0) + jnp.log1p(jnp.exp(-jnp.abs(x)))


def _bf16_pieces(x):
    hi = x.astype(BF16)
    r1 = x - hi.astype(F32)
    mid = r1.astype(BF16)
    lo = (r1 - mid.astype(F32)).astype(BF16)
    return [hi, mid, lo]


def _resident(shape):
    nd = len(shape)
    return pl.BlockSpec(shape, lambda *_: (0,) * nd, pipeline_mode=pl.Buffered(1))


def _resident_columns(arr, start, width):
    assert start % width == 0 and width % LANES == 0
    return pl.BlockSpec((arr.shape[0], width), lambda *_: (0, start // width), pipeline_mode=pl.Buffered(1))


def _params(sem):
    return pltpu.CompilerParams(dimension_semantics=sem, vmem_limit_bytes=VMEM_LIMIT_BYTES)


def _memkv_kernel(mem_ref, g_ref, w_ref, k_ref, v_ref, kb_ref, vb_ref):
    h = _rms(mem_ref[...], g_ref[...]).astype(BF16)
    kv = _dot(h, w_ref[...])
    k = kv[:, :D_MEM]
    v = kv[:, D_MEM:]
    k_ref[...] = k
    v_ref[...] = v
    kb_ref[...] = k.astype(BF16)
    vb_ref[...] = v.astype(BF16)


def _memory_kv(mem2d, g, w_kv):
    rows = mem2d.shape[0]
    tm = 512
    assert rows % tm == 0
    row_spec = pl.BlockSpec((tm, D_MODEL), lambda i: (i, 0))
    out_spec = pl.BlockSpec((tm, D_MEM), lambda i: (i, 0))
    return pl.pallas_call(
        _memkv_kernel,
        grid=(rows // tm,),
        in_specs=[row_spec, _resident((1, D_MODEL)), _resident((D_MODEL, 2 * D_MEM))],
        out_specs=[out_spec] * 4,
        out_shape=[jax.ShapeDtypeStruct((rows, D_MEM), F32)] * 2
        + [jax.ShapeDtypeStruct((rows, D_MEM), BF16)] * 2,
        compiler_params=_params(("arbitrary",)),
        name="memory_kv",
    )(mem2d, g, w_kv)


def _conv_span(tl):
    blk = SUBLANES * CONV_STEP
    return 2 * blk if tl % (2 * blk) == 0 else blk


def _strided_depthwise(load, put, w_ref, lane_tile, width, r0, span, finish):
    blk = SUBLANES * CONV_STEP
    starts = [s0 + j for s0 in range(r0, r0 + span, blk) for j in range(CONV_STEP)]
    accs = [w_ref[lane_tile, width]] * len(starts)
    for k in range(width):
        wk = w_ref[lane_tile, k]
        accs = [a + wk * load(s0 + k, CONV_STEP) for a, s0 in zip(accs, starts)]
    for a, s0 in zip(accs, starts):
        put(s0, CONV_STEP, finish(a))


def _strided_rows(start, stride):
    return pl.ds(start, SUBLANES, stride=stride)


def _tap_table(w, b):
    wb = jnp.concatenate([w, b[None, :]], axis=0).astype(F32)
    wb = wb.reshape(wb.shape[0], -1, 1, LANES).transpose(1, 0, 2, 3)
    return jnp.broadcast_to(wb, wb.shape[:2] + (SUBLANES, LANES))


def _conv_mem_kernel(x_ref, hist_ref, k_ref, v_ref, g_ref, wglu_ref, wgc_ref, wq_ref, wgm_ref,
                     dww_ref, clng_ref, clnb_ref, wco_ref, wmo_ref,
                     p_ref, hist_out_ref, ubuf, ybuf, act):
    nseq, tl = x_ref.shape[0], x_ref.shape[1]
    rows = nseq * tl
    t = pl.program_id(1)
    n_tiles = D_CONV // LANES
    h = _rms(x_ref[...].reshape(rows, D_MODEL), g_ref[...]).astype(BF16)

    glu = _dot(h, wglu_ref[...])
    u = glu[:, :D_CONV] * _sigmoid(glu[:, D_CONV:])

    @pl.when(t == 0)
    def _():
        for b in range(nseq):
            for ct in range(n_tiles):
                ubuf[b, ct, CONV_PAD - CONV_HIST:CONV_PAD, :] = hist_ref[b, :, ct * LANES:(ct + 1) * LANES]

    span = _conv_span(tl)
    for b in range(nseq):
        for ct in range(n_tiles):
            ubuf[b, ct, CONV_PAD:CONV_PAD + tl, :] = u[b * tl:(b + 1) * tl, ct * LANES:(ct + 1) * LANES]
        for r0 in range(0, tl, span):
            for ct in range(n_tiles):
                def load(start, stride, b=b, ct=ct):
                    return ubuf[b, ct, _strided_rows(CONV_PAD - CONV_HIST + start, stride), :]

                def put(start, stride, val, b=b, ct=ct):
                    ybuf[ct, _strided_rows(b * tl + start, stride), :] = val

                _strided_depthwise(load, put, dww_ref, ct, CONV_WIDTH, r0, span, lambda a: a)
            for r1 in range(b * tl + r0, b * tl + r0 + span, ROW_BLOCK):
                y = jnp.concatenate([ybuf[ct, r1:r1 + ROW_BLOCK, :] for ct in range(n_tiles)], axis=-1)
                mu = jnp.mean(y, axis=-1, keepdims=True)
                yc = y - mu
                var = jnp.mean(yc * yc, axis=-1, keepdims=True)
                yn = yc * lax.rsqrt(var + EPS) * clng_ref[...] + clnb_ref[...]
                act[r1:r1 + ROW_BLOCK, :] = _silu(yn).astype(BF16)
        for ct in range(n_tiles):
            new_hist = ubuf[b, ct, tl + CONV_PAD - CONV_HIST:tl + CONV_PAD, :]
            ubuf[b, ct, CONV_PAD - CONV_HIST:CONV_PAD, :] = new_hist
            hist_out_ref[b, :, ct * LANES:(ct + 1) * LANES] = new_hist

    out_conv = _dot(act[...], wco_ref[...])
    merged = _sigmoid(_dot(h, wgc_ref[...])) * out_conv

    q = _dot(h, wq_ref[...]).astype(BF16)
    per_seq = []
    for b in range(nseq):
        heads = []
        for hd in range(MEM_HEADS):
            sl = slice(hd * MEM_HEAD_DIM, (hd + 1) * MEM_HEAD_DIM)
            s = _dot_nt(q[b * tl:(b + 1) * tl, sl], k_ref[b, :, sl]) * (MEM_HEAD_DIM ** -0.5)
            e = jnp.exp(s - jnp.max(s, axis=-1, keepdims=True))
            heads.append(_dot(e.astype(BF16), v_ref[b, :, sl]) / jnp.sum(e, axis=-1, keepdims=True))
        per_seq.append(jnp.concatenate(heads, axis=-1))
    o = jnp.concatenate(per_seq, axis=0).astype(BF16)
    out_mem = _dot(o, wmo_ref[...])
    merged = merged + _sigmoid(_dot(h, wgm_ref[...])) * out_mem
    p_ref[...] = merged.reshape(nseq, tl, D_MODEL)


def _conv_mem_branch(x, hist, k_b, v_b, w, nseq, tl):
    bsz, seqlen, _ = x.shape
    rows = nseq * tl
    assert bsz % nseq == 0 and seqlen % tl == 0 and tl >= CONV_HIST
    assert tl % ROW_BLOCK == 0 and tl % (SUBLANES * CONV_STEP) == 0
    tile = pl.BlockSpec((nseq, tl, D_MODEL), lambda b, t: (b, t, 0))
    per_seq = lambda r, c: pl.BlockSpec((nseq, r, c), lambda b, t: (b, 0, 0))
    weights = [w['g_mix_pre'], w['w_in'], w['w_gate_conv'], w['w_q'], w['w_gate_mem'],
               w['dw_taps'], w['cln_g'], w['cln_b'], w['w_conv_out'], w['w_mem_out']]
    weight_specs = [_resident(a.shape) for a in weights]
    weight_specs[1] = _resident_columns(w['w_in'], 0, N_GLU)
    return pl.pallas_call(
        _conv_mem_kernel,
        grid=(bsz // nseq, seqlen // tl),
        in_specs=[tile, per_seq(CONV_HIST, D_CONV), per_seq(N_MEM, D_MEM), per_seq(N_MEM, D_MEM)] + weight_specs,
        out_specs=[tile, per_seq(CONV_HIST, D_CONV)],
        out_shape=[jax.ShapeDtypeStruct(x.shape, F32),
                   jax.ShapeDtypeStruct((bsz, CONV_HIST, D_CONV), F32)],
        scratch_shapes=[pltpu.VMEM((nseq, D_CONV // LANES, CONV_PAD + tl, LANES), F32),
                        pltpu.VMEM((D_CONV // LANES, rows, LANES), F32),
                        pltpu.VMEM((rows, D_CONV), BF16)],
        compiler_params=_params(("arbitrary", "arbitrary")),
        name="conv_mem_branch",
    )(x, hist, k_b, v_b, *weights)


def _ssd_merge_kernel(q, x_ref, p_ref, hist_ref, state_in_ref, g_ref, wz_ref, wxbc_ref, wdt_ref, wdtt_ref,
                      wgs_ref, cw_ref, dtb_row_ref, dtb_col_ref, alog_row_ref, alog_col_ref,
                      dskip_ref, ng_ref, wso_ref, wo_ref, gpost_ref, tri3_ref, tri3t_ref, e3_ref,
                      xo_ref, hist_out_ref, state_ref, cbuf, xs_s, b_s, c_s, y_s, et_s, spread_s, src_s):
    nseq, tl = x_ref.shape[0], x_ref.shape[1]
    rows = nseq * tl
    t = pl.program_id(1)
    n_tiles = SSM_CONV_DIM // LANES
    n_x_tiles = D_INNER // LANES
    n_b_tiles = SSM_GROUPS * SSM_STATE // LANES
    n_blocks = rows // q
    blocks_per_seq = tl // q
    slot = SSM_HEAD_DIM
    x = x_ref[...].reshape(rows, D_MODEL)
    h = _rms(x, g_ref[...]).astype(BF16)

    v = {}

    def stage_proj():
        v['dt_raw'] = _dot(h, wdt_ref[...])
        v['dtt_raw'] = _dot_nt(wdtt_ref[...], h)

    def stage_softplus():
        a_row = -jnp.exp(alog_row_ref[...]) * LOG2_E
        a_col = -jnp.exp(alog_col_ref[...]) * LOG2_E
        v['dt'] = _softplus(v['dt_raw'] + dtb_row_ref[...])
        v['dt_t'] = _softplus(v['dtt_raw'] + dtb_col_ref[...])
        v['da'] = v['dt'] * a_row
        v['da_t'] = v['dt_t'] * a_col

    def stage_cumsum():
        v['acum'] = _dot(tri3_ref[...], jnp.concatenate(_bf16_pieces(v['da']), axis=0))
        v['acum_t'] = _dot(jnp.concatenate(_bf16_pieces(v['da_t']), axis=-1), tri3t_ref[...])

    def stage_spread():
        spread_s[0] = jnp.concatenate(_bf16_pieces(v['dt']), axis=-1)
        spread_s[1] = jnp.concatenate(_bf16_pieces(v['acum']), axis=-1)
        v['src_t'] = v['acum_t'] - jnp.log2(v['dt_t'])
        for c in range(n_blocks):
            tot = jnp.exp2(jnp.sum(v['da_t'][:, c * q:(c + 1) * q], axis=1, keepdims=True))
            tot = jnp.broadcast_to(tot, (SSM_HEADS, SSM_STATE))
            for g in range(SSM_GROUPS):
                et_s[c, g] = tot[g * SSM_HPG:(g + 1) * SSM_HPG, :]

    def stage_rows(c):
        def run():
            for g in range(SSM_GROUPS):
                pieces = []
                for r in range(SSM_HPG):
                    hh = g * SSM_HPG + r
                    pieces.append(v['src_t'][hh:hh + 1, c * q:(c + 1) * q])
                    if q < slot:
                        pieces.append(jnp.zeros((1, slot - q), F32))
                src_s[c, g] = jnp.concatenate(pieces, axis=-1)
        return run

    stages = [stage_proj, stage_softplus, stage_cumsum, stage_spread] + [stage_rows(c) for c in range(n_blocks)]

    @pl.when(t == 0)
    def _():
        for b in range(nseq):
            for ct in range(n_tiles):
                cbuf[b, ct, SSM_PAD - SSM_HIST:SSM_PAD, :] = hist_ref[b, :, ct * LANES:(ct + 1) * LANES]
        state_ref[...] = state_in_ref[...]

    def xbc_dest(ct):
        if ct < n_x_tiles:
            return xs_s, ct
        if ct < n_x_tiles + n_b_tiles:
            return b_s, ct - n_x_tiles
        return c_s, ct - n_x_tiles - n_b_tiles

    xbc = _dot(h, wxbc_ref[...])
    span = _conv_span(tl)
    for ct in range(n_tiles):
        cs = slice(ct * LANES, (ct + 1) * LANES)
        dst, di = xbc_dest(ct)
        for b in range(nseq):
            cbuf[b, ct, SSM_PAD:SSM_PAD + tl, :] = xbc[b * tl:(b + 1) * tl, cs]

            def load(start, stride, b=b):
                return cbuf[b, ct, _strided_rows(SSM_PAD - SSM_HIST + start, stride), :]

            def put(start, stride, val, b=b):
                dst[di, _strided_rows(b * tl + start, stride), :] = val

            for r0 in range(0, tl, span):
                _strided_depthwise(load, put, cw_ref, ct, SSM_CONV_WIDTH, r0, span, _silu)
            new_hist = cbuf[b, ct, tl + SSM_PAD - SSM_HIST:tl + SSM_PAD, :]
            cbuf[b, ct, SSM_PAD - SSM_HIST:SSM_PAD, :] = new_hist
            hist_out_ref[b, :, cs] = new_hist
        if ct % STAGE_EVERY == STAGE_EVERY - 1 and ct // STAGE_EVERY < len(stages):
            stages[ct // STAGE_EVERY]()
    for stage in stages[n_tiles // STAGE_EVERY:]:
        stage()

    li = lax.broadcasted_iota(jnp.int32, (q, SSM_GROUP_DIM), 0)
    si = lax.broadcasted_iota(jnp.int32, (q, SSM_GROUP_DIM), 1) % slot
    causal = li >= si
    rr = lax.broadcasted_iota(jnp.int32, (SSM_GROUP_DIM, SSM_GROUP_DIM), 0) // slot
    rc = lax.broadcasted_iota(jnp.int32, (SSM_GROUP_DIM, SSM_GROUP_DIM), 1) // SSM_HEAD_DIM
    same_head = rr == rc

    def pad_rows(a):
        if q == slot:
            return a
        return jnp.concatenate([a, jnp.zeros((slot - q, a.shape[1]), a.dtype)], axis=0)

    def block(c, carry):
        b = 0 if nseq == 1 else c // blocks_per_seq
        blk = pl.ds(pl.multiple_of(c * q, q), q)
        ex = _dot(jnp.concatenate([spread_s[0, blk, :], spread_s[1, blk, :]], axis=0), e3_ref[...])
        cgs = [c_s[g, blk, :].astype(BF16) for g in range(SSM_GROUPS)]
        bgs = [b_s[g, blk, :].astype(BF16) for g in range(SSM_GROUPS)]
        sts = [state_ref[b, g] for g in range(SSM_GROUPS)]
        cbs = [_dot_nt(cg, jnp.concatenate([pad_rows(bg)] * SSM_HPG, axis=0)) for cg, bg in zip(cgs, bgs)]
        yis = [_dot_nt(cg, st.astype(BF16)) for cg, st in zip(cgs, sts)]
        ms, yps, xds, xgs = [], [], [], []
        for g in range(SSM_GROUPS):
            gs = slice(g * SSM_GROUP_DIM, (g + 1) * SSM_GROUP_DIM)
            de = ex[:q, gs]
            ae = ex[q:, gs]
            total = ae[q - 1:q, :]
            lmat = jnp.exp2(jnp.where(causal, ae - src_s[c, g], NEG_BIG))
            ms.append((cbs[g] * lmat).astype(BF16))
            xg = jnp.concatenate([xs_s[2 * g, blk, :], xs_s[2 * g + 1, blk, :]], axis=-1)
            xgs.append(xg)
            yps.append(jnp.exp2(ae) * yis[g] + dskip_ref[g] * xg)
            xds.append((xg * (de * jnp.exp2(total - ae))).astype(BF16))
        for g in range(SSM_GROUPS):
            xb = pad_rows(xgs[g].astype(BF16))
            xdiag = jnp.where(same_head, jnp.concatenate([xb] * SSM_HPG, axis=0), jnp.zeros_like(xb[:1, :1]))
            y_s[g, blk, :] = _dot(ms[g], xdiag) + yps[g]
            et = et_s[c, g]
            decay = jnp.concatenate([jnp.broadcast_to(et[r:r + 1, :], (SSM_HEAD_DIM, SSM_STATE))
                                     for r in range(SSM_HPG)], axis=0)
            state_ref[b, g] = sts[g] * decay + _dot_tn(xds[g], bgs[g])
        return carry

    lax.fori_loop(0, n_blocks, block, 0, unroll=2 if n_blocks % 2 == 0 else 1)

    z = _silu(_dot(h, wz_ref[...]))
    parts = []
    for g in range(SSM_GROUPS):
        yg = y_s[g] * z[:, g * SSM_GROUP_DIM:(g + 1) * SSM_GROUP_DIM]
        parts.append(yg * lax.rsqrt(jnp.mean(yg * yg, axis=-1, keepdims=True) + EPS))
    y = (jnp.concatenate(parts, axis=-1) * ng_ref[...]).astype(BF16)
    out_ssm = _dot(y, wso_ref[...])
    merged = p_ref[...].reshape(rows, D_MODEL) + _sigmoid(_dot(h, wgs_ref[...])) * out_ssm
    mix = _dot(merged.astype(BF16), wo_ref[...])
    xo_ref[...] = (x + _rms(mix, gpost_ref[...])).reshape(nseq, tl, D_MODEL)


def _ssd_constants(rows, q):
    i = jnp.arange(rows)
    tri = ((i[:, None] // q == i[None, :] // q) & (i[None, :] <= i[:, None])).astype(BF16)
    tri3 = jnp.concatenate([tri] * 3, axis=1)
    tri3t = jnp.concatenate([tri.T] * 3, axis=0)
    head_of_lane = jnp.arange(D_INNER) // SSM_HEAD_DIM
    e = (jnp.arange(SSM_HEADS)[:, None] == head_of_lane[None, :]).astype(BF16)
    e3 = jnp.concatenate([e] * 3, axis=0)
    return tri3, tri3t, e3


def _ssd_merge_branch(x, p, hist, state, w, nseq, tl, q):
    bsz, seqlen, _ = x.shape
    rows = nseq * tl
    assert bsz % nseq == 0 and seqlen % tl == 0 and tl % q == 0 and q <= SSM_HEAD_DIM
    assert tl % (SUBLANES * CONV_STEP) == 0
    tile = pl.BlockSpec((nseq, tl, D_MODEL), lambda b, t: (b, t, 0))
    per_seq = lambda r, c: pl.BlockSpec((nseq, r, c), lambda b, t: (b, 0, 0))
    state_spec = pl.BlockSpec((nseq, SSM_GROUPS, SSM_GROUP_DIM, SSM_STATE), lambda b, t: (b, 0, 0, 0))
    tri3, tri3t, e3 = _ssd_constants(rows, q)
    weights = [w['g_mix_pre'], w['w_in'], w['w_in'], w['w_dt'], w['w_dt_t'], w['w_gate_ssm'],
               w['ssm_taps'], w['dt_bias_row'], w['dt_bias_col'], w['a_log_row'],
               w['a_log_col'], w['d_skip_wide'], w['ssm_norm_g'], w['w_ssm_out'], w['w_o'], w['g_mix_post'],
               tri3, tri3t, e3]
    weight_specs = [_resident(a.shape) for a in weights]
    weight_specs[1] = _resident_columns(w['w_in'], OFF_Z, D_INNER)
    weight_specs[2] = _resident_columns(w['w_in'], OFF_XBC, SSM_CONV_DIM)
    return pl.pallas_call(
        functools.partial(_ssd_merge_kernel, q),
        grid=(bsz // nseq, seqlen // tl),
        in_specs=[tile, tile, per_seq(SSM_HIST, SSM_CONV_DIM), state_spec] + weight_specs,
        out_specs=[tile, per_seq(SSM_HIST, SSM_CONV_DIM), state_spec],
        out_shape=[jax.ShapeDtypeStruct(x.shape, F32),
                   jax.ShapeDtypeStruct((bsz, SSM_HIST, SSM_CONV_DIM), F32),
                   jax.ShapeDtypeStruct((bsz, SSM_GROUPS, SSM_GROUP_DIM, SSM_STATE), F32)],
        scratch_shapes=[pltpu.VMEM((nseq, SSM_CONV_DIM // LANES, SSM_PAD + tl, LANES), F32),
                        pltpu.VMEM((D_INNER // LANES, rows, LANES), F32),
                        pltpu.VMEM((SSM_GROUPS, rows, SSM_STATE), F32),
                        pltpu.VMEM((SSM_GROUPS, rows, SSM_STATE), F32),
                        pltpu.VMEM((SSM_GROUPS, rows, SSM_GROUP_DIM), F32),
                        pltpu.VMEM((rows // q, SSM_GROUPS, SSM_HPG, SSM_STATE), F32),
                        pltpu.VMEM((2, rows, 3 * SSM_HEADS), BF16),
                        pltpu.VMEM((rows // q, SSM_GROUPS, 1, SSM_GROUP_DIM), F32)],
        compiler_params=_params(("arbitrary", "arbitrary")),
        name="ssd_merge_branch",
    )(x, p, hist, state, *weights)


def _ffn_kernel(x_ref, gpre_ref, wg_ref, wu_ref, wd_ref, gpost_ref, o_ref):
    x = x_ref[...]
    hn = _rms(x, gpre_ref[...]).astype(BF16)
    a = (_silu(_dot(hn, wg_ref[...])) * _dot(hn, wu_ref[...])).astype(BF16)
    o_ref[...] = x + _rms(_dot(a, wd_ref[...]), gpost_ref[...])


def _ffn(x2d, w, tm):
    rows = x2d.shape[0]
    assert rows % tm == 0
    tile = pl.BlockSpec((tm, D_MODEL), lambda i: (i, 0))
    weights = [w['g_ffn_pre'], w['w_gate'], w['w_up'], w['w_down'], w['g_ffn_post']]
    return pl.pallas_call(
        _ffn_kernel,
        grid=(rows // tm,),
        in_specs=[tile] + [_resident(a.shape) for a in weights],
        out_specs=tile,
        out_shape=jax.ShapeDtypeStruct(x2d.shape, F32),
        compiler_params=_params(("arbitrary",)),
        name="ffn",
    )(x2d, *weights)


def _tile_rows(bsz, seqlen, long_rows, short_rows):
    if seqlen % long_rows == 0:
        return long_rows, 1
    nseq = max(short_rows // seqlen, 1)
    while bsz % nseq:
        nseq -= 1
    return seqlen, nseq


def _layer(x, k_b, v_b, conv_hist, ssm_hist, ssm_state, w, q):
    bsz, seqlen, _ = x.shape
    tl, nseq = _tile_rows(bsz, seqlen, CONV_TILE_ROWS, CONV_SHORT_ROWS)
    p, new_conv = _conv_mem_branch(x, conv_hist, k_b, v_b, w, nseq, tl)
    tl, nseq = _tile_rows(bsz, seqlen, SSD_TILE_ROWS, SSD_SHORT_ROWS)
    state = ssm_state.reshape(bsz, SSM_GROUPS, SSM_GROUP_DIM, SSM_STATE)
    x1, new_ssm_conv, new_state = _ssd_merge_branch(x, p, ssm_hist, state, w, nseq, tl, q)
    x2 = _ffn(x1.reshape(bsz * seqlen, D_MODEL), w, min(512, bsz * seqlen)).reshape(x.shape)
    return x2, new_conv, new_ssm_conv, new_state.reshape(bsz, SSM_HEADS, SSM_HEAD_DIM, SSM_STATE)


def _layer_weights(i, g_mix_pre, w_in, dw_w, dw_b, cln_g, cln_b, w_conv_out, ssm_conv_w, ssm_conv_b,
                   dt_bias, a_log, d_skip, ssm_norm_g, w_ssm_out, w_mem_out, w_o, g_mix_post, g_ffn_pre,
                   w_gate, w_up, w_down, g_ffn_post):
    row = lambda a: a[i].reshape(1, -1).astype(F32)
    col = lambda a: a[i].reshape(-1, 1).astype(F32)
    win = lambda a, b: w_in[i, :, a:b].astype(BF16)
    return {
        'g_mix_pre': row(g_mix_pre),
        'w_in': w_in[i].astype(BF16),
        'w_dt': win(OFF_DT, OFF_Q),
        'w_dt_t': win(OFF_DT, OFF_Q).T,
        'w_q': win(OFF_Q, OFF_GATE),
        'w_gate_conv': win(OFF_GATE, OFF_GATE + D_MODEL),
        'w_gate_ssm': win(OFF_GATE + D_MODEL, OFF_GATE + 2 * D_MODEL),
        'w_gate_mem': win(OFF_GATE + 2 * D_MODEL, OFF_GATE + 3 * D_MODEL),
        'dw_taps': _tap_table(dw_w[i], dw_b[i]), 'cln_g': row(cln_g), 'cln_b': row(cln_b),
        'w_conv_out': w_conv_out[i].astype(BF16),
        'ssm_taps': _tap_table(ssm_conv_w[i], ssm_conv_b[i]),
        'dt_bias_row': row(dt_bias), 'dt_bias_col': col(dt_bias),
        'a_log_row': row(a_log), 'a_log_col': col(a_log),
        'd_skip_wide': jnp.repeat(d_skip[i].astype(F32), SSM_HEAD_DIM).reshape(SSM_GROUPS, 1, SSM_GROUP_DIM),
        'ssm_norm_g': row(ssm_norm_g),
        'w_ssm_out': w_ssm_out[i].astype(BF16),
        'w_mem_out': w_mem_out[i].astype(BF16),
        'w_o': w_o[i].astype(BF16),
        'g_mix_post': row(g_mix_post), 'g_ffn_pre': row(g_ffn_pre),
        'w_gate': w_gate[i].astype(BF16), 'w_up': w_up[i].astype(BF16), 'w_down': w_down[i].astype(BF16),
        'g_ffn_post': row(g_ffn_post),
    }


def kernel(x_prompt, x_sample, mem_prompt, cache_mem_k, cache_mem_v, state_conv, state_ssm_conv, state_ssm, g_mix_pre, w_in, dw_w, dw_b, cln_g, cln_b, w_conv_out, ssm_conv_w, ssm_conv_b, dt_bias, a_log, d_skip, ssm_norm_g, w_ssm_out, mem_norm_g, w_mem_k, w_mem_v, w_mem_out, w_o, g_mix_post, g_ffn_pre, w_gate, w_up, w_down, g_ffn_post):
    depth = w_in.shape[0]
    bp, lp, _ = x_prompt.shape
    bs, ls, _ = x_sample.shape
    q_p = SSD_BLOCK if lp % SSD_BLOCK == 0 else lp
    q_s = SSD_BLOCK if ls % SSD_BLOCK == 0 else ls
    xp, xs = x_prompt, x_sample
    mem2d = mem_prompt.reshape(bp * N_MEM, D_MODEL)
    zc = jnp.zeros((bp, CONV_HIST, D_CONV), F32)
    zsc = jnp.zeros((bp, SSM_HIST, SSM_CONV_DIM), F32)
    zs = jnp.zeros((bp, SSM_HEADS, SSM_HEAD_DIM, SSM_STATE), F32)
    outs = [[] for _ in range(8)]
    for i in range(depth):
        w = _layer_weights(i, g_mix_pre, w_in, dw_w, dw_b, cln_g, cln_b, w_conv_out, ssm_conv_w, ssm_conv_b,
                           dt_bias, a_log, d_skip, ssm_norm_g, w_ssm_out, w_mem_out, w_o, g_mix_post,
                           g_ffn_pre, w_gate, w_up, w_down, g_ffn_post)
        w_kv = jnp.concatenate([w_mem_k[i], w_mem_v[i]], axis=1).astype(BF16)
        mk, mv, mk_b, mv_b = _memory_kv(mem2d, mem_norm_g[i].reshape(1, D_MODEL).astype(F32), w_kv)
        xp, c_new, sc_new, s_new = _layer(xp, mk_b.reshape(bp, N_MEM, D_MEM), mv_b.reshape(bp, N_MEM, D_MEM),
                                          zc, zsc, zs, w, q_p)
        outs[0].append(mk.reshape(bp, N_MEM, MEM_HEADS, MEM_HEAD_DIM))
        outs[1].append(mv.reshape(bp, N_MEM, MEM_HEADS, MEM_HEAD_DIM))
        outs[2].append(c_new)
        outs[3].append(sc_new)
        outs[4].append(s_new)
        ck = cache_mem_k[i].reshape(bs, N_MEM, D_MEM).astype(BF16)
        cv = cache_mem_v[i].reshape(bs, N_MEM, D_MEM).astype(BF16)
        xs, c_new, sc_new, s_new = _layer(xs, ck, cv, state_conv[i], state_ssm_conv[i], state_ssm[i], w, q_s)
        outs[5].append(c_new)
        outs[6].append(sc_new)
        outs[7].append(s_new)
    return (xp, xs) + tuple(jnp.stack(o) for o in outs)
```
